```python
import math
import jax, jax.numpy as jnp
from jax import lax
import numpy as np

D_MODEL = 1024
BATCH = 4
SEQ = 4096
DEPTH = 1
DEC_BATCH = 32
DEC_SEQ = 4
PAST_LEN = 8192
PAGE_SIZE = 128

NSA_HEADS = 16
NSA_GROUPS = 4
NSA_HPG = NSA_HEADS // NSA_GROUPS
NSA_DK = D_MODEL // NSA_HEADS
CMP_BLOCK = 32
CMP_STRIDE = 16
CMP_HID = 4 * NSA_DK
SEL_BLOCK = 64
N_SEL = 16
WINDOW = 512
FORCE_BONUS = 1e4
N_KV_SLOTS = 4
DIFF_HEADS = 8
DIFF_DD = D_MODEL // (2 * DIFF_HEADS)
DIFF_DV = 2 * DIFF_DD
D_FF = 4 * D_MODEL
NUM_BUCKETS = 32
MAX_DISTANCE = 128
Q_BLOCK = 128
EPS = 1e-6
NEG_INF = -1e30
TINY = 1e-30
SPLIT_SIZES = (NSA_HEADS * NSA_DK, 6 * NSA_GROUPS * NSA_DK, 3 * NSA_HEADS,
               2 * DIFF_HEADS * DIFF_DD, 2 * DIFF_HEADS * DIFF_DD, DIFF_HEADS * DIFF_DV,
               D_MODEL, D_MODEL)
SPLIT_POINTS = tuple(int(v) for v in np.cumsum(SPLIT_SIZES)[:-1])
N_IN = int(sum(SPLIT_SIZES))

kernel_name = "nsa_diffattn_gated_hybrid_step"


def rmsnorm(x, g):
    xf = x.astype(jnp.float32)
    y = xf * lax.rsqrt(jnp.mean(xf * xf, axis=-1, keepdims=True) + EPS)
    return (y * g.astype(jnp.float32)).astype(x.dtype)


def masked_softmax(s, mask):
    s = jnp.where(mask, s, NEG_INF)
    m = jnp.max(s, axis=-1, keepdims=True)
    p = jnp.where(mask, jnp.exp(s - m), 0.0)
    return p / jnp.maximum(jnp.sum(p, axis=-1, keepdims=True), TINY)


def t5_bucket(dist):
    n = jnp.maximum(dist, 0)
    max_exact = NUM_BUCKETS // 2
    nf = jnp.maximum(n, 1).astype(jnp.float32)
    log_ratio = jnp.log(nf / max_exact) / math.log(MAX_DISTANCE / max_exact)
    large = max_exact + (log_ratio * (NUM_BUCKETS - max_exact)).astype(jnp.int32)
    large = jnp.minimum(large, NUM_BUCKETS - 1)
    return jnp.where(n < max_exact, n, large)


def sweep_query_blocks(fn, T):
    qb = Q_BLOCK if T % Q_BLOCK == 0 else T
    starts = jnp.arange(T // qb, dtype=jnp.int32) * qb
    out = lax.map(lambda s: fn(s, qb), starts)
    out = jnp.moveaxis(out, 0, 1)
    return out.reshape((out.shape[0], T) + out.shape[3:])


def compress_rows(rows, pe, w1, w2):
    B, L = rows.shape[:2]
    R = CMP_BLOCK // CMP_STRIDE
    nch = -(-L // CMP_STRIDE)
    rows = jnp.pad(rows, ((0, 0), (0, nch * CMP_STRIDE - L), (0, 0), (0, 0)))
    chunks = rows.reshape(B, nch, CMP_STRIDE, NSA_GROUPS, NSA_DK)
    w1r = w1.reshape(R, CMP_STRIDE, NSA_DK, CMP_HID)
    nc = nch - R + 1
    h = jnp.einsum('ld,ldh->h', pe, w1.reshape(CMP_BLOCK, NSA_DK, CMP_HID))
    for r in range(R):
        part = jnp.einsum('bcjgd,jdh->bcgh', chunks, w1r[r])
        h = h + part[:, r:r + nc]
    return jnp.einsum('bcgh,hd->bcgd', jax.nn.gelu(h), w2)


def cmp_sel_weights(nc, n_slc):
    c0 = np.arange(nc)[:, None] * CMP_STRIDE
    s0 = np.arange(n_slc)[None, :] * SEL_BLOCK
    shared = np.clip(np.minimum(c0 + CMP_BLOCK, s0 + SEL_BLOCK) - np.maximum(c0, s0), 0, None)
    return jnp.asarray(shared / CMP_BLOCK, dtype=jnp.float32)


def nsa_mixer(q, gates, kv_all, win_all, tab_nsa, pe_k, w1_k, w2_k, pe_v, w1_v, w2_v, k_gain):
    B, T = q.shape[:2]
    L = kv_all.shape[1]
    t0 = L - T
    scale = NSA_DK ** -0.5
    kc = rmsnorm(compress_rows(kv_all[:, :, 0], pe_k, w1_k, w2_k), k_gain)
    vc = compress_rows(kv_all[:, :, 1], pe_v, w1_v, w2_v)
    nc = kc.shape[1]
    cmp_end = jnp.arange(nc, dtype=jnp.int32) * CMP_STRIDE + (CMP_BLOCK - 1)
    n_slc = -(-L // SEL_BLOCK)
    pad = n_slc * SEL_BLOCK - L

    def to_blocks(r):
        r = jnp.pad(r, ((0, 0), (0, pad), (0, 0), (0, 0)))
        return r.reshape(B, n_slc, SEL_BLOCK, NSA_GROUPS, NSA_DK).transpose(0, 3, 1, 2, 4)

    ksb = to_blocks(kv_all[:, :, 2])
    vsb = to_blocks(kv_all[:, :, 3])
    sel_w = cmp_sel_weights(nc, n_slc)
    n_top = min(N_SEL, n_slc)
    blk_ids = jnp.arange(n_slc, dtype=jnp.int32)
    tok_ids = jnp.arange(SEL_BLOCK, dtype=jnp.int32)
    k_win, v_win = win_all[:, :, 0], win_all[:, :, 1]
    qg = q.reshape(B, T, NSA_GROUPS, NSA_HPG, NSA_DK)
    gg = gates.reshape(B, T, NSA_GROUPS, NSA_HPG, 3)
    tab_g = tab_nsa.reshape(NUM_BUCKETS, NSA_GROUPS, NSA_HPG)
    gather = jax.vmap(jax.vmap(lambda blocks, ix: blocks[ix]))
    g_ids = jnp.arange(NSA_GROUPS)[None, :, None, None]

    def head_bias(dist):
        return tab_g[t5_bucket(dist)].astype(jnp.float32).transpose(2, 3, 0, 1)

    def block(s, qb):
        qbk = lax.dynamic_slice_in_dim(qg, s, qb, axis=1)
        gb = lax.dynamic_slice_in_dim(gg, s, qb, axis=1)
        tq = t0 + s + jnp.arange(qb, dtype=jnp.int32)
        dc = tq[:, None] - cmp_end[None, :]
        s_c = jnp.einsum('bqghd,bcgd->bghqc', qbk, kc).astype(jnp.float32) * scale + head_bias(dc)
        p_c = masked_softmax(s_c, dc >= 0)
        o_c = jnp.einsum('bghqc,bcgd->bqghd', p_c.astype(vc.dtype), vc)
        imp = jnp.einsum('bghqc,cn->bgqn', p_c, sel_w)
        cur = tq[:, None] // SEL_BLOCK
        forced = (blk_ids[None] == 0) | (blk_ids[None] == cur) | (blk_ids[None] == cur - 1)
        valid = blk_ids[None] * SEL_BLOCK <= tq[:, None]
        score = jnp.where(valid, imp + jnp.where(forced, FORCE_BONUS, 0.0), -FORCE_BONUS)
        _, idx = lax.top_k(score, n_top)
        nk = n_top * SEL_BLOCK
        kg = gather(ksb, idx).reshape(B, NSA_GROUPS, qb, nk, NSA_DK)
        vg = gather(vsb, idx).reshape(B, NSA_GROUPS, qb, nk, NSA_DK)
        pos = (idx[..., None] * SEL_BLOCK + tok_ids).reshape(B, NSA_GROUPS, qb, nk)
        ds = tq[None, None, :, None] - pos
        bias_s = tab_g[t5_bucket(ds), g_ids].astype(jnp.float32).transpose(0, 1, 4, 2, 3)
        s_s = jnp.einsum('bqghd,bgqkd->bghqk', qbk, kg).astype(jnp.float32) * scale + bias_s
        p_s = masked_softmax(s_s, (ds >= 0)[:, :, None])
        o_s = jnp.einsum('bghqk,bgqkd->bqghd', p_s.astype(vg.dtype), vg)
        lw = WINDOW + qb
        kw = lax.dynamic_slice_in_dim(k_win, s, lw, axis=1)
        vw = lax.dynamic_slice_in_dim(v_win, s, lw, axis=1)
        kpos = t0 - WINDOW + s + jnp.arange(lw, dtype=jnp.int32)
        dw = tq[:, None] - kpos[None, :]
        mask_w = (kpos[None, :] >= 0) & (dw >= 0) & (dw < WINDOW)
        s_w = jnp.einsum('bqghd,bkgd->bghqk', qbk, kw).astype(jnp.float32) * scale + head_bias(dw)
        p_w = masked_softmax(s_w, mask_w)
        o_w = jnp.einsum('bghqk,bkgd->bqghd', p_w.astype(vw.dtype), vw)
        return gb[..., 0:1] * o_c + gb[..., 1:2] * o_s + gb[..., 2:3] * o_w

    o = sweep_query_blocks(block, T)
    return o.reshape(B, T, NSA_HEADS * NSA_DK)


def diff_mixer(q1, q2, k1, k2, v, lam, lam_init, subln_g, tab_diff):
    B, T = q1.shape[:2]
    L = k1.shape[1]
    t0 = L - T
    scale = DIFF_DD ** -0.5
    kpos = jnp.arange(L, dtype=jnp.int32)

    def block(s, qb):
        q1b = lax.dynamic_slice_in_dim(q1, s, qb, axis=1)
        q2b = lax.dynamic_slice_in_dim(q2, s, qb, axis=1)
        tq = t0 + s + jnp.arange(qb, dtype=jnp.int32)
        dist = tq[:, None] - kpos[None, :]
        bias = tab_diff[t5_bucket(dist)].astype(jnp.float32).transpose(2, 0, 1)
        mask = dist >= 0
        s1 = jnp.einsum('bqhd,bkhd->bhqk', q1b, k1).astype(jnp.float32) * scale + bias
        s2 = jnp.einsum('bqhd,bkhd->bhqk', q2b, k2).astype(jnp.float32) * scale + bias
        a = masked_softmax(s1, mask) - lam * masked_softmax(s2, mask)
        return jnp.einsum('bhqk,bkhd->bqhd', a.astype(v.dtype), v)

    o = sweep_query_blocks(block, T)
    o = rmsnorm(o, subln_g) * (1.0 - lam_init)
    return o.reshape(B, T, DIFF_HEADS * DIFF_DV)


def decoder_layer(x, past_nsa, past_diff, past_win, win_keep, lam_init, lam, tab,
                  attn_g, w_in, q_g, k_g, pe_k, w1_k, w2_k, pe_v, w1_v, w2_v,
                  dq_g, dk_g, subln_g, w_o, mlp_g, w_up, w_down):
    B, T, _ = x.shape
    xn = rmsnorm(x, attn_g)
    proj = jnp.einsum('btd,de->bte', xn, w_in)
    q_n, kv_n, g_n, q_d, k_d, v_d, ga, gb = jnp.split(proj, SPLIT_POINTS, axis=-1)
    q = rmsnorm(q_n.reshape(B, T, NSA_HEADS, NSA_DK), q_g)
    kv = kv_n.reshape(B, T, 6, NSA_GROUPS, NSA_DK)
    k_sel = rmsnorm(kv[:, :, 2], k_g)
    k_win = rmsnorm(kv[:, :, 4], k_g)
    new_nsa = jnp.stack([kv[:, :, 0], kv[:, :, 1], k_sel, kv[:, :, 3]], axis=2)
    new_win = jnp.stack([k_win, kv[:, :, 5]], axis=2)
    gates = jax.nn.sigmoid(g_n.reshape(B, T, NSA_HEADS, 3))
    kv_all = jnp.concatenate([past_nsa, new_nsa], axis=1)
    win_all = jnp.concatenate([past_win, new_win], axis=1)
    o_a = nsa_mixer(q, gates, kv_all, win_all, tab[:, :NSA_HEADS],
                    pe_k, w1_k, w2_k, pe_v, w1_v, w2_v, k_g)
    qd = rmsnorm(q_d.reshape(B, T, DIFF_HEADS, 2, DIFF_DD), dq_g)
    kd = rmsnorm(k_d.reshape(B, T, DIFF_HEADS, 2, DIFF_DD), dk_g)
    vd = v_d.reshape(B, T, DIFF_HEADS, DIFF_DV)
    new_diff = jnp.stack([kd.reshape(B, T, DIFF_HEADS, 2 * DIFF_DD), vd], axis=2)
    diff_all = jnp.concatenate([past_diff, new_diff], axis=1)
    L = diff_all.shape[1]
    k_all = diff_all[:, :, 0].reshape(B, L, DIFF_HEADS, 2, DIFF_DD)
    o_b = diff_mixer(qd[:, :, :, 0], qd[:, :, :, 1], k_all[:, :, :, 0], k_all[:, :, :, 1],
                     diff_all[:, :, 1], lam, lam_init, subln_g, tab[:, NSA_HEADS:])
    o = jax.nn.sigmoid(ga) * o_a + jax.nn.sigmoid(gb) * o_b
    h = x + jnp.einsum('bte,ed->btd', o, w_o)
    hn = rmsnorm(h, mlp_g)
    y = h + jnp.einsum('btf,fd->btd', jnp.square(jax.nn.relu(jnp.einsum('btd,df->btf', hn, w_up))), w_down)
    return y, new_nsa, new_diff, win_all[:, win_all.shape[1] - win_keep:]


def setup_inputs(seed: int = 0) -> dict:
    key = jax.random.key(seed)
    ks = jax.random.split(key, 32)
    n_pages = PAST_LEN // PAGE_SIZE
    in_use = DEC_BATCH * n_pages
    n_phys = in_use + max(1, in_use // 4)
    win_buf = min(WINDOW, PAST_LEN)
    f32 = jnp.float32

    def nrm(k, shape, s):
        return jax.random.normal(k, shape, f32) * s

    def gain(k, shape):
        return 1.0 + 0.05 * jax.random.normal(k, shape, f32)

    page_table = jax.random.permutation(ks[3], n_phys)[:in_use].reshape(DEC_BATCH, n_pages).astype(jnp.int32)
    return {
        'x_prompt': nrm(ks[0], (BATCH, SEQ, D_MODEL), 1.0),
        'x_sample': nrm(ks[1], (DEC_BATCH, DEC_SEQ, D_MODEL), 1.0),
        'cache_nsa_kv': nrm(ks[2], (DEPTH, n_phys, PAGE_SIZE, N_KV_SLOTS, NSA_GROUPS, NSA_DK), 1.0),
        'cache_diff_kv': nrm(ks[4], (DEPTH, n_phys, PAGE_SIZE, 2, DIFF_HEADS, DIFF_DV), 1.0),
        'cache_nsa_win': nrm(ks[5], (DEPTH, DEC_BATCH, win_buf, 2, NSA_GROUPS, NSA_DK), 1.0),
        'page_table': page_table,
        'rel_bias_table': nrm(ks[6], (NUM_BUCKETS, NSA_HEADS + DIFF_HEADS), 0.2),
        'attn_norm': gain(ks[7], (DEPTH, D_MODEL)),
        'w_in': nrm(ks[8], (DEPTH, D_MODEL, N_IN), D_MODEL ** -0.5),
        'nsa_q_gain': gain(ks[9], (DEPTH, NSA_DK)),
        'nsa_k_gain': gain(ks[10], (DEPTH, NSA_DK)),
        'cmp_pe_k': nrm(ks[11], (DEPTH, CMP_BLOCK, NSA_DK), 0.1),
        'cmp_w1_k': nrm(ks[12], (DEPTH, CMP_BLOCK * NSA_DK, CMP_HID), (CMP_BLOCK * NSA_DK) ** -0.5),
        'cmp_w2_k': nrm(ks[13], (DEPTH, CMP_HID, NSA_DK), CMP_HID ** -0.5),
        'cmp_pe_v': nrm(ks[14], (DEPTH, CMP_BLOCK, NSA_DK), 0.1),
        'cmp_w1_v': nrm(ks[15], (DEPTH, CMP_BLOCK * NSA_DK, CMP_HID), (CMP_BLOCK * NSA_DK) ** -0.5),
        'cmp_w2_v': nrm(ks[16], (DEPTH, CMP_HID, NSA_DK), CMP_HID ** -0.5),
        'diff_q_gain': gain(ks[17], (DEPTH, DIFF_DD)),
        'diff_k_gain': gain(ks[18], (DEPTH, DIFF_DD)),
        'lambda_q1': nrm(ks[19], (DEPTH, DIFF_DD), 0.1),
        'lambda_k1': nrm(ks[20], (DEPTH, DIFF_DD), 0.1),
        'lambda_q2': nrm(ks[21], (DEPTH, DIFF_DD), 0.1),
        'lambda_k2': nrm(ks[22], (DEPTH, DIFF_DD), 0.1),
        'diff_subln': gain(ks[23], (DEPTH, DIFF_DV)),
        'w_o': nrm(ks[24], (DEPTH, D_MODEL, D_MODEL), D_MODEL ** -0.5),
        'mlp_norm': gain(ks[25], (DEPTH, D_MODEL)),
        'w_up': nrm(ks[26], (DEPTH, D_MODEL, D_FF), D_MODEL ** -0.5),
        'w_down': nrm(ks[27], (DEPTH, D_FF, D_MODEL), D_FF ** -0.5),
    }


def reference(x_prompt, x_sample, cache_nsa_kv, cache_diff_kv, cache_nsa_win, page_table,
              rel_bias_table, attn_norm, w_in, nsa_q_gain, nsa_k_gain,
              cmp_pe_k, cmp_w1_k, cmp_w2_k, cmp_pe_v, cmp_w1_v, cmp_w2_v,
              diff_q_gain, diff_k_gain, lambda_q1, lambda_k1, lambda_q2, lambda_k2, diff_subln,
              w_o, mlp_norm, w_up, w_down):
    B, T = x_prompt.shape[:2]
    DB = x_sample.shape[0]
    n_pages = page_table.shape[1]
    past_len = n_pages * cache_nsa_kv.shape[2]
    win_buf = cache_nsa_win.shape[2]
    yp, ys = x_prompt, x_sample
    nsa_p_l, diff_p_l, win_p_l, nsa_s_l, diff_s_l, win_s_l = [], [], [], [], [], []
    for l in range(DEPTH):
        lam_init = 0.8 - 0.6 * math.exp(-0.3 * l)
        lam = (jnp.exp(jnp.sum(lambda_q1[l].astype(jnp.float32) * lambda_k1[l].astype(jnp.float32)))
               - jnp.exp(jnp.sum(lambda_q2[l].astype(jnp.float32) * lambda_k2[l].astype(jnp.float32)))
               + lam_init)
        lw = (lam_init, lam, rel_bias_table, attn_norm[l], w_in[l], nsa_q_gain[l], nsa_k_gain[l],
              cmp_pe_k[l], cmp_w1_k[l], cmp_w2_k[l], cmp_pe_v[l], cmp_w1_v[l], cmp_w2_v[l],
              diff_q_gain[l], diff_k_gain[l], diff_subln[l], w_o[l], mlp_norm[l], w_up[l], w_down[l])
        p_nsa0 = jnp.zeros((B, 0, N_KV_SLOTS, NSA_GROUPS, NSA_DK), x_prompt.dtype)
        p_diff0 = jnp.zeros((B, 0, 2, DIFF_HEADS, DIFF_DV), x_prompt.dtype)
        p_win0 = jnp.zeros((B, WINDOW, 2, NSA_GROUPS, NSA_DK), x_prompt.dtype)
        yp, nsa_p, diff_p, win_p = decoder_layer(yp, p_nsa0, p_diff0, p_win0, min(WINDOW, T), *lw)
        s_nsa = cache_nsa_kv[l][page_table].reshape(DB, past_len, N_KV_SLOTS, NSA_GROUPS, NSA_DK)
        s_diff = cache_diff_kv[l][page_table].reshape(DB, past_len, 2, DIFF_HEADS, DIFF_DV)
        s_win = jnp.pad(cache_nsa_win[l], ((0, 0), (WINDOW - win_buf, 0), (0, 0), (0, 0), (0, 0)))
        ys, nsa_s, diff_s, win_s = decoder_layer(ys, s_nsa, s_diff, s_win, win_buf, *lw)
        nsa_p_l.append(nsa_p); diff_p_l.append(diff_p); win_p_l.append(win_p)
        nsa_s_l.append(nsa_s); diff_s_l.append(diff_s); win_s_l.append(win_s)
    return (yp, ys, jnp.stack(nsa_p_l, 0), jnp.stack(diff_p_l, 0), jnp.stack(win_p_l, 0),
            jnp.stack(nsa_s_l, 0), jnp.stack(diff_s_l, 0), jnp.stack(win_s_l, 0))
```

```python
import functools
import math

import numpy as np
import jax
import jax.numpy as jnp
from jax import lax
from jax.experimental import pallas as pl
from jax.experimental.pallas import tpu as pltpu

F32 = jnp.float32
BF16 = jnp.bfloat16

D_MODEL = 1024
NSA_HEADS = 16
NSA_GROUPS = 4
NSA_HPG = NSA_HEADS // NSA_GROUPS
NSA_DK = D_MODEL // NSA_HEADS
GD = NSA_GROUPS * NSA_DK
CMP_BLOCK = 32
CMP_STRIDE = 16
CMP_HID = 4 * NSA_DK
SEL_BLOCK = 64
N_SEL = 16
WINDOW = 512
FORCE_BONUS = 1e4
DIFF_HEADS = 8
DIFF_DD = D_MODEL // (2 * DIFF_HEADS)
DIFF_DV = 2 * DIFF_DD
D_FF = 4 * D_MODEL
NUM_BUCKETS = 32
MAX_DISTANCE = 128
EPS = 1e-6
NEG_INF = -1e30
TINY = 1e-30
QK_SCALE = NSA_DK ** -0.5

LANE = 128
MXU_DIM = 256
VMEM_LIMIT = 56 * 1024 * 1024
PAGE = 128

C_QN = 0
C_QD = C_QN + D_MODEL
C_KD = C_QD + D_MODEL
C_VD = C_KD + D_MODEL
C_GA = C_VD + D_MODEL
C_GB = C_GA + D_MODEL
C_G3 = C_GB + D_MODEL
N_ROWMAJOR = C_G3 + LANE


def _cparams(sem):
    return pltpu.CompilerParams(dimension_semantics=sem, vmem_limit_bytes=VMEM_LIMIT)


def _t5_bucket_np(dist):
    n = np.maximum(dist, 0)
    max_exact = NUM_BUCKETS // 2
    nf = np.maximum(n, 1).astype(np.float32)
    log_ratio = np.log(nf / np.float32(max_exact)) / np.float32(math.log(MAX_DISTANCE / max_exact))
    large = max_exact + (log_ratio * np.float32(NUM_BUCKETS - max_exact)).astype(np.int32)
    large = np.minimum(large, NUM_BUCKETS - 1)
    return np.where(n < max_exact, n, large).astype(np.int32)


def _segment_mean_matrix():
    i = np.arange(MXU_DIM)
    return jnp.asarray((i[:, None] // NSA_DK == i[None, :] // NSA_DK) / NSA_DK, dtype=BF16)


def _dot_t(a, b):
    return lax.dot_general(a, b, (((1,), (1,)), ((), ())), preferred_element_type=F32)


def _proj_kernel(x_ref, g_ref, w_ref, wt_ref, bd_ref, gains_ref, kgt_ref,
                 q_ref, nsat_ref, wint_ref, g3_ref, qd_ref, diff_ref, ga_ref, gb_ref):
    x = x_ref[0]
    xn = x * lax.rsqrt(jnp.mean(x * x, axis=-1, keepdims=True) + EPS) * g_ref[...]
    xn = xn.astype(BF16)
    bd = bd_ref[...]

    def mm(c0, width=MXU_DIM):
        return jnp.dot(xn, w_ref[:, c0:c0 + width], preferred_element_type=F32)

    def headnorm(acc, gain_row):
        ms = jnp.dot((acc * acc).astype(BF16), bd, preferred_element_type=F32)
        return acc * lax.rsqrt(ms + EPS) * gains_ref[gain_row:gain_row + 1, :]

    heads_per_tile = MXU_DIM // DIFF_DV
    for c in range(D_MODEL // MXU_DIM):
        o = c * MXU_DIM
        q_ref[0, :, o:o + MXU_DIM] = (headnorm(mm(C_QN + o), 0) * QK_SCALE).astype(BF16)
        qd_ref[0, :, o:o + MXU_DIM] = (headnorm(mm(C_QD + o), 1) * QK_SCALE).astype(BF16)
        kd = headnorm(mm(C_KD + o), 2)
        vd = mm(C_VD + o)
        for hh in range(heads_per_tile):
            h = c * heads_per_tile + hh
            diff_ref[0, :, 0, h, :] = kd[:, hh * DIFF_DV:(hh + 1) * DIFF_DV]
            diff_ref[0, :, 1, h, :] = vd[:, hh * DIFF_DV:(hh + 1) * DIFF_DV]
        ga_ref[0, :, o:o + MXU_DIM] = jax.nn.sigmoid(mm(C_GA + o))
        gb_ref[0, :, o:o + MXU_DIM] = jax.nn.sigmoid(mm(C_GB + o))
    g3_ref[0] = jax.nn.sigmoid(mm(C_G3, LANE))

    def mm_t(slot):
        return _dot_t(wt_ref[slot * GD:(slot + 1) * GD, :], xn)

    def headnorm_t(acc):
        ms = jnp.dot(bd, (acc * acc).astype(BF16), preferred_element_type=F32)
        return acc * lax.rsqrt(ms + EPS) * kgt_ref[...]

    nsat_ref[0, 0:GD, :] = mm_t(0)
    nsat_ref[0, GD:2 * GD, :] = mm_t(1)
    nsat_ref[0, 2 * GD:3 * GD, :] = headnorm_t(mm_t(2))
    nsat_ref[0, 3 * GD:4 * GD, :] = mm_t(3)
    wint_ref[0, 0:GD, :] = headnorm_t(mm_t(4))
    wint_ref[0, GD:2 * GD, :] = mm_t(5)


def _proj(x, attn_g, w_row, w_kvt, bd, gains, kgain_t, tm):
    nb, t, _ = x.shape
    row = lambda width: pl.BlockSpec((1, tm, width), lambda b, i: (b, i, 0))
    col = lambda feats: pl.BlockSpec((1, feats, tm), lambda b, i: (b, 0, i))
    const = lambda shape: pl.BlockSpec(shape, lambda b, i: (0, 0), pipeline_mode=pl.Buffered(1))
    out_specs = (row(D_MODEL), col(4 * GD), col(2 * GD), row(LANE), row(D_MODEL),
                 pl.BlockSpec((1, tm, 2, DIFF_HEADS, DIFF_DV), lambda b, i: (b, i, 0, 0, 0)),
                 row(D_MODEL), row(D_MODEL))
    out_shapes = (
        jax.ShapeDtypeStruct((nb, t, D_MODEL), BF16),
        jax.ShapeDtypeStruct((nb, 4 * GD, t), F32),
        jax.ShapeDtypeStruct((nb, 2 * GD, t), F32),
        jax.ShapeDtypeStruct((nb, t, LANE), F32),
        jax.ShapeDtypeStruct((nb, t, D_MODEL), BF16),
        jax.ShapeDtypeStruct((nb, t, 2, DIFF_HEADS, DIFF_DV), F32),
        jax.ShapeDtypeStruct((nb, t, D_MODEL), F32),
        jax.ShapeDtypeStruct((nb, t, D_MODEL), F32),
    )
    return pl.pallas_call(
        _proj_kernel,
        grid=(nb, t // tm),
        in_specs=[row(D_MODEL), const((1, D_MODEL)), const(w_row.shape), const(w_kvt.shape),
                  const(bd.shape), const(gains.shape), const(kgain_t.shape)],
        out_specs=out_specs,
        out_shape=out_shapes,
        compiler_params=_cparams(("parallel", "parallel")),
        name="proj",
    )(x, attn_g, w_row, w_kvt, bd, gains, kgain_t)


def _split_w_in(w_in):
    sizes = (D_MODEL, 6 * GD, 3 * NSA_HEADS, D_MODEL, D_MODEL, D_MODEL, D_MODEL, D_MODEL)
    pts = np.cumsum(sizes)[:-1]
    q_n, kv_n, g_n, q_d, k_d, v_d, ga, gb = jnp.split(w_in, pts, axis=-1)
    g_pad = jnp.pad(g_n, ((0, 0), (0, LANE - 3 * NSA_HEADS)))
    w_row = jnp.concatenate([q_n, q_d, k_d, v_d, ga, gb, g_pad], axis=-1).astype(BF16)
    return w_row, kv_n.T.astype(BF16)


CH_PER_PAGE = PAGE // CMP_STRIDE
J_PER_TILE = MXU_DIM // NSA_DK


def _pe_kernel(pe_ref, w1_ref, o_ref):
    for kv in range(2):
        o_ref[kv] = jnp.dot(pe_ref[kv], w1_ref[kv], preferred_element_type=F32,
                            precision=lax.Precision.HIGHEST)


def _pe_terms(pe2, w1):
    return pl.pallas_call(
        _pe_kernel,
        out_shape=jax.ShapeDtypeStruct((2, 8, CMP_HID), F32),
        name="cmp_pe",
    )(pe2, w1)


def _compress_kernel(pps, *refs):
    page_refs = refs[1:1 + pps]
    next_ref, tail_ref, w1_ref, h0_ref, w2_ref, bd_ref, kg_ref, kc_ref, vc_ref = refs[1 + pps:10 + pps]
    rows_refs = refs[10 + pps:]
    is_last = pl.program_id(1) == pl.num_programs(1) - 1
    cb = pps * CH_PER_PAGE
    rows = cb + CH_PER_PAGE
    gpl = LANE // NSA_DK
    for fb, rows_ref in enumerate(rows_refs):
        feats = slice(fb * LANE, (fb + 1) * LANE)
        for k, p in enumerate(page_refs):
            rows_ref[k * PAGE:(k + 1) * PAGE, :] = p[0, feats, :].T
        rows_ref[pps * PAGE:(pps + 1) * PAGE, :] = jnp.where(is_last, tail_ref[0, feats, :], next_ref[0, feats, :]).T

    for kv in range(2):
        acc = [None, None]
        for jt in range(CMP_STRIDE // J_PER_TILE):
            xs = [[rows_refs[kv * (GD // LANE) + half][pl.ds(jt * J_PER_TILE + jj, rows, stride=CMP_STRIDE), :]
                   for half in range(GD // LANE)] for jj in range(J_PER_TILE)]
            lhs = jnp.concatenate(
                [jnp.concatenate([x[g // gpl][:, (g % gpl) * NSA_DK:(g % gpl + 1) * NSA_DK] for x in xs], axis=1)
                 for g in range(NSA_GROUPS)], axis=0).astype(BF16)
            for r in range(2):
                k0 = r * CMP_STRIDE * NSA_DK + jt * MXU_DIM
                part = jnp.dot(lhs, w1_ref[kv, k0:k0 + MXU_DIM, :], preferred_element_type=F32)
                acc[r] = part if acc[r] is None else acc[r] + part
        out = None
        for g in range(NSA_GROUPS):
            h = h0_ref[kv, 0:1, :] + acc[0][g * rows:g * rows + cb] + acc[1][g * rows + 1:g * rows + 1 + cb]
            part = jnp.dot(jax.nn.gelu(h).astype(BF16), w2_ref[kv, g], preferred_element_type=F32)
            out = part if out is None else out + part
        if kv == 0:
            ms = jnp.dot((out * out).astype(BF16), bd_ref[...], preferred_element_type=F32)
            kc_ref[0] = out * lax.rsqrt(ms + EPS) * kg_ref[...]
        else:
            vc_ref[0] = out


def _compress(pages, page_index, nb, npg, tail, w1, h0, w2z, bd, kgain, page_table=None):
    pps = min(8, npg)
    assert npg % pps == 0
    prefetch = [page_table if page_table is not None else jnp.zeros((1, 1), jnp.int32)]
    blk = (1, 2 * GD, PAGE)
    page_spec = lambda k: pl.BlockSpec(blk, lambda b, s, pt: page_index(b, s * pps + k, pt))
    next_spec = pl.BlockSpec(blk, lambda b, s, pt: page_index(b, jnp.minimum(s * pps + pps, npg - 1), pt))
    const = lambda shape: pl.BlockSpec(shape, lambda b, s, pt: (0,) * len(shape), pipeline_mode=pl.Buffered(1))
    out_spec = pl.BlockSpec((1, pps * CH_PER_PAGE, GD), lambda b, s, pt: (b, s, 0))
    out_sds = jax.ShapeDtypeStruct((nb, npg * CH_PER_PAGE, GD), F32)
    return pl.pallas_call(
        functools.partial(_compress_kernel, pps),
        grid_spec=pltpu.PrefetchScalarGridSpec(
            num_scalar_prefetch=1,
            grid=(nb, npg // pps),
            in_specs=[page_spec(k) for k in range(pps)] + [
                next_spec,
                pl.BlockSpec(blk, lambda b, s, pt: (b, 0, 0)),
                const(w1.shape), const(h0.shape), const(w2z.shape), const(bd.shape), const(kgain.shape)],
            out_specs=(out_spec, out_spec),
            scratch_shapes=[pltpu.VMEM(((pps + 1) * PAGE, LANE), F32)] * (2 * GD // LANE)),
        out_shape=(out_sds, out_sds),
        compiler_params=_cparams(("parallel", "arbitrary")),
        name="compress",
    )(*prefetch, *([pages] * (pps + 1)), tail, w1, h0, w2z, bd, kgain)


CMP_WIN = 64
CMP_WIN_BACK = 32
SCORE_PAD = -3.0e38


def _gate_col(g3, head, branch):
    c = head * 3 + branch
    return g3[:, c:c + 1]


def _cmp_topk_kernel(t0, tq, ncp, nc, n_slc, nsp, n_top,
                     q_ref, kc_ref, vc_ref, g3_ref, bwin_ref, far_ref, selw_ref, o_ref, msk_ref):
    qpos0 = t0 + pl.program_id(1) * tq
    col = lax.broadcasted_iota(jnp.int32, (tq, ncp), 1)
    qpos = qpos0 + lax.broadcasted_iota(jnp.int32, (tq, ncp), 0)
    valid_c = (col * CMP_STRIDE + (CMP_BLOCK - 1) <= qpos) & (col < nc)
    c0 = qpos0 // CMP_STRIDE - CMP_WIN_BACK
    in_win = (col >= c0) & (col < c0 + CMP_WIN)
    place = (lax.broadcasted_iota(jnp.int32, (CMP_WIN, ncp), 1)
             == lax.broadcasted_iota(jnp.int32, (CMP_WIN, ncp), 0) + c0).astype(F32)
    g3 = g3_ref[0]
    blk = lax.broadcasted_iota(jnp.int32, (tq, nsp), 1)
    bpos = qpos0 + lax.broadcasted_iota(jnp.int32, (tq, nsp), 0)
    cur = bpos // SEL_BLOCK
    forced = (blk == 0) | (blk == cur) | (blk == cur - 1)
    valid_b = blk * SEL_BLOCK <= bpos
    outs = []
    for g in range(NSA_GROUPS):
        kc_g = kc_ref[0, :, g * NSA_DK:(g + 1) * NSA_DK].astype(BF16)
        vc_g = vc_ref[0, :, g * NSA_DK:(g + 1) * NSA_DK].astype(BF16)
        psum = jnp.zeros((tq, ncp), F32)
        for hl in range(NSA_HPG):
            h = g * NSA_HPG + hl
            q_h = q_ref[0, :, h * NSA_DK:(h + 1) * NSA_DK]
            s = _dot_t(q_h, kc_g)
            placed = jnp.dot(bwin_ref[h], place, preferred_element_type=F32, precision=lax.Precision.HIGHEST)
            s = s + jnp.where(in_win, placed, far_ref[h])
            s = jnp.where(valid_c, s, NEG_INF)
            m = jnp.max(s, axis=-1, keepdims=True)
            p = jnp.where(valid_c, jnp.exp(s - m), 0.0)
            p = p / jnp.maximum(jnp.sum(p, axis=-1, keepdims=True), TINY)
            o_h = jnp.dot(p.astype(BF16), vc_g, preferred_element_type=F32)
            outs.append(o_h * _gate_col(g3, h, 0))
            psum = psum + p
        imp = jnp.dot(psum, selw_ref[...], preferred_element_type=F32, precision=lax.Precision.HIGHEST)
        score = jnp.where(valid_b, imp + jnp.where(forced, FORCE_BONUS, 0.0), -FORCE_BONUS)
        score = jnp.where(blk < n_slc, score, SCORE_PAD)

        def pick(_, carry):
            sc, chosen = carry
            best = jnp.max(sc, axis=-1, keepdims=True)
            first = jnp.min(jnp.where(sc == best, blk, nsp), axis=-1, keepdims=True)
            hit = blk == first
            return jnp.where(hit, SCORE_PAD, sc), jnp.where(hit, 1.0, chosen)

        _, chosen = lax.fori_loop(0, n_top, pick, (score, jnp.zeros((tq, nsp), F32)))
        msk_ref[0, :, g * nsp:(g + 1) * nsp] = chosen.astype(BF16)
    o_ref[0] = jnp.concatenate(outs, axis=1)


def _cmp_topk(q, kc, vc, g3, bwin, far, selw, *, t0, tq, nc, n_slc, n_top):
    nb, tt, _ = q.shape
    ncp = kc.shape[1]
    nsp = selw.shape[1]
    const = lambda shape: pl.BlockSpec(shape, lambda b, i: (0,) * len(shape), pipeline_mode=pl.Buffered(1))
    per_b = lambda shape: pl.BlockSpec(shape, lambda b, i: (b, 0, 0))
    tile = lambda width: pl.BlockSpec((1, tq, width), lambda b, i: (b, i, 0))
    return pl.pallas_call(
        functools.partial(_cmp_topk_kernel, t0, tq, ncp, nc, n_slc, nsp, n_top),
        grid=(nb, tt // tq),
        in_specs=[tile(D_MODEL), per_b((1, ncp, GD)), per_b((1, ncp, GD)), tile(LANE),
                  const(bwin.shape), const(far.shape), const(selw.shape)],
        out_specs=(tile(D_MODEL), tile(NSA_GROUPS * nsp)),
        out_shape=(jax.ShapeDtypeStruct((nb, tt, D_MODEL), F32),
                   jax.ShapeDtypeStruct((nb, tt, NSA_GROUPS * nsp), BF16)),
        compiler_params=_cparams(("parallel", "arbitrary")),
        name="cmp_topk",
    )(q, kc, vc, g3, bwin, far, selw)


def _cmp_tables(tab_nsa, tq, ncp, n_slc, nsp):
    r = np.arange(tq)[:, None]
    w = np.arange(CMP_WIN)[None, :]
    dc = r - CMP_STRIDE * (w - CMP_WIN_BACK) - (CMP_BLOCK - 1)
    bwin = jnp.transpose(tab_nsa[_t5_bucket_np(dc)], (2, 0, 1))
    far = tab_nsa[NUM_BUCKETS - 1].reshape(NSA_HEADS, 1, 1)
    c_lo = np.arange(ncp)[:, None] * CMP_STRIDE
    s_lo = np.arange(nsp)[None, :] * SEL_BLOCK
    shared = np.clip(np.minimum(c_lo + CMP_BLOCK, s_lo + SEL_BLOCK) - np.maximum(c_lo, s_lo), 0, None)
    shared = np.where(np.arange(nsp)[None, :] < n_slc, shared, 0)
    return bwin, far, jnp.asarray(shared / CMP_BLOCK, dtype=F32)


BIAS_DIAG, BIAS_PREV, BIAS_FAR = 0, 1, 2
T_I, T_J, T_FIRST, T_LAST, T_BIAS, T_NEW, T_KPOS = range(7)


def _pair_tables(nq, tq, t0, tkp, tkn):
    assert t0 % tkp == 0
    n_past = t0 // tkp
    rows = []
    for i in range(nq):
        q0 = t0 + i * tq
        tiles = [(0, j, j * tkp, tkp) for j in range(n_past)]
        tiles += [(1, j, t0 + j * tkn, tkn) for j in range((i * tq + tq - 1) // tkn + 1)]
        for n, (is_new, j, k0, tk) in enumerate(tiles):
            delta = q0 - k0
            assert delta in (0, tk) or delta >= tk + MAX_DISTANCE - 1, (tq, t0, tkp, tkn)
            bsel = BIAS_DIAG if delta == 0 else BIAS_PREV if delta == tk else BIAS_FAR
            rows.append((i, j, int(n == 0), int(n == len(tiles) - 1), bsel, is_new, k0))
    return jnp.asarray(np.array(rows, dtype=np.int32).T.copy())


def _toeplitz_bias(tab, tq, tk):
    r = np.arange(tq)[:, None]
    c = np.arange(tk)[None, :]
    diag = jnp.transpose(tab[_t5_bucket_np(r - c)], (2, 0, 1))
    prev = jnp.transpose(tab[_t5_bucket_np(r - c + tk)], (2, 0, 1))
    return jnp.stack([diag, prev], axis=1)


def _tile_bias(bias_ref, head, bsel, far):
    return jnp.where(bsel == BIAS_FAR, far, bias_ref[head, jnp.minimum(bsel, BIAS_PREV)])


def _online_softmax_step(s, valid, pv, m_ref, l_ref, acc_ref):
    s = jnp.where(valid, s, NEG_INF)
    m_prev = m_ref[:, 0:1]
    m_next = jnp.maximum(m_prev, jnp.max(s, axis=-1, keepdims=True))
    alpha = jnp.exp(m_prev - m_next)
    p = jnp.where(valid, jnp.exp(s - m_next), 0.0)
    l_ref[...] = jnp.broadcast_to(alpha * l_ref[:, 0:1] + jnp.sum(p, axis=-1, keepdims=True), l_ref.shape)
    m_ref[...] = jnp.broadcast_to(m_next, m_ref.shape)
    acc_ref[...] = alpha * acc_ref[...] + pv(p.astype(BF16))


def _flash_prologue(tab_ref, p, m_ref, l_ref, acc_ref):
    @pl.when(tab_ref[T_FIRST, p] == 1)
    def _():
        m_ref[...] = jnp.full(m_ref.shape, NEG_INF, F32)
        l_ref[...] = jnp.zeros(l_ref.shape, F32)
        acc_ref[...] = jnp.zeros(acc_ref.shape, F32)


def _flash_specs(tq):
    qtile = lambda width: pl.BlockSpec((1, tq, width), lambda b, p, tab, *_: (b, tab[T_I, p], 0))
    const = lambda shape: pl.BlockSpec(shape, lambda b, p, tab, *_: (0,) * len(shape), pipeline_mode=pl.Buffered(1))
    return qtile, const


def _past_page(tab, pt, b, p, ppt, k):
    n_past = pt.shape[1] // ppt
    j = jnp.where(tab[T_NEW, p] == 1, n_past - 1, tab[T_J, p])
    return pt[b, j * ppt + k]


def _sel_kernel(t0, tq, ppt, has_past, nsp, *refs):
    tab_ref = refs[0]
    refs = refs[2 if has_past else 1:]
    q_ref, msk_ref, g3_ref, far_ref, bn_ref, kn_ref, vn_ref = refs[:7]
    refs = refs[7:]
    if has_past:
        bp_ref, kp_refs, vp_refs = refs[0], refs[1:1 + ppt], refs[1 + ppt:1 + 2 * ppt]
        refs = refs[1 + 2 * ppt:]
    o_ref, m_ref, l_ref, acc_ref = refs
    p = pl.program_id(1)
    i, bsel, kpos0 = tab_ref[T_I, p], tab_ref[T_BIAS, p], tab_ref[T_KPOS, p]
    _flash_prologue(tab_ref, p, m_ref, l_ref, acc_ref)

    def step(kt_all, vt_all, bias_ref):
        tk = kt_all.shape[1]
        kpos = kpos0 + lax.broadcasted_iota(jnp.int32, (tq, tk), 1)
        qpos = t0 + i * tq + lax.broadcasted_iota(jnp.int32, (tq, tk), 0)
        blk_of_key = (kpos0 + lax.broadcasted_iota(jnp.int32, (nsp, tk), 1)) // SEL_BLOCK
        expand = (lax.broadcasted_iota(jnp.int32, (nsp, tk), 0) == blk_of_key).astype(BF16)
        causal = kpos <= qpos
        for g in range(NSA_GROUPS):
            chosen = jnp.dot(msk_ref[0, :, g * nsp:(g + 1) * nsp], expand, preferred_element_type=F32)
            valid = causal & (chosen > 0.5)
            valid = jnp.concatenate([valid] * NSA_HPG, axis=0)
            q_g = jnp.concatenate(
                [q_ref[0, :, (g * NSA_HPG + hl) * NSA_DK:(g * NSA_HPG + hl + 1) * NSA_DK] for hl in range(NSA_HPG)],
                axis=0)
            kt_g = kt_all[g * NSA_DK:(g + 1) * NSA_DK, :].astype(BF16)
            vt_g = vt_all[g * NSA_DK:(g + 1) * NSA_DK, :].astype(BF16)
            s = jnp.dot(q_g, kt_g, preferred_element_type=F32)
            s = s + _tile_bias(bias_ref, g, bsel, far_ref[g])
            _online_softmax_step(s, valid, lambda pb: _dot_t(pb, vt_g), m_ref.at[g], l_ref.at[g], acc_ref.at[g])

    if has_past:
        @pl.when(tab_ref[T_NEW, p] == 0)
        def _():
            step(jnp.concatenate([r[0] for r in kp_refs], axis=1),
                 jnp.concatenate([r[0] for r in vp_refs], axis=1), bp_ref)

        @pl.when(tab_ref[T_NEW, p] == 1)
        def _():
            step(kn_ref[0], vn_ref[0], bn_ref)
    else:
        step(kn_ref[0], vn_ref[0], bn_ref)

    @pl.when(tab_ref[T_LAST, p] == 1)
    def _():
        g3 = g3_ref[0]
        outs = []
        for g in range(NSA_GROUPS):
            o = acc_ref[g] / jnp.maximum(l_ref[g, :, 0:1], TINY)
            for hl in range(NSA_HPG):
                outs.append(o[hl * tq:(hl + 1) * tq] * _gate_col(g3, g * NSA_HPG + hl, 1))
        o_ref[0] = jnp.concatenate(outs, axis=1)


def _sel_attn(q, msk, g3, far, bias_new, new_t, *, t0, tq, tkn, past=None):
    nb, tt, _ = q.shape
    nsp = msk.shape[2] // NSA_GROUPS
    rows = NSA_HPG * tq
    tkp = past[2] if past else tkn
    ppt = tkp // PAGE
    tab = _pair_tables(tt // tq, tq, t0, tkp, tkn)
    qtile, const = _flash_specs(tq)
    new_spec = lambda slot: pl.BlockSpec((1, GD, tkn), lambda b, p, tab, *_: (b, slot, tab[T_J, p] * tab[T_NEW, p]))
    in_specs = [qtile(D_MODEL), qtile(NSA_GROUPS * nsp), qtile(LANE), const(far.shape), const(bias_new.shape),
                new_spec(2), new_spec(3)]
    args = [q, msk, g3, far, bias_new, new_t, new_t]
    prefetch = [tab]
    if past:
        page_spec = lambda slot, k: pl.BlockSpec(
            (1, GD, PAGE), lambda b, p, tab, pt: (_past_page(tab, pt, b, p, ppt, k), slot, 0))
        in_specs += [const(past[3].shape)] + [page_spec(s, k) for s in (2, 3) for k in range(ppt)]
        args += [past[3]] + [past[1]] * (2 * ppt)
        prefetch.append(past[0])
    return pl.pallas_call(
        functools.partial(_sel_kernel, t0, tq, ppt, past is not None, nsp),
        grid_spec=pltpu.PrefetchScalarGridSpec(
            num_scalar_prefetch=len(prefetch),
            grid=(nb, tab.shape[1]),
            in_specs=in_specs,
            out_specs=qtile(D_MODEL),
            scratch_shapes=[pltpu.VMEM((NSA_GROUPS, rows, LANE), F32), pltpu.VMEM((NSA_GROUPS, rows, LANE), F32),
                            pltpu.VMEM((NSA_GROUPS, rows, NSA_DK), F32)]),
        out_shape=jax.ShapeDtypeStruct((nb, tt, D_MODEL), F32),
        compiler_params=_cparams(("parallel", "arbitrary")),
        name="sel_attn",
    )(*prefetch, *args)


def _win_kernel(t0, tq, n_piece, *refs):
    q_ref, g3_ref, bias_ref = refs[:3]
    piece_refs = refs[3:3 + n_piece]
    o_ref = refs[3 + n_piece]
    cols = [r[0] for r in piece_refs]
    kvt = cols[0] if n_piece == 1 else jnp.concatenate(cols, axis=1)
    nk = kvt.shape[1]
    qpos0 = t0 + pl.program_id(1) * tq
    r = lax.broadcasted_iota(jnp.int32, (tq, nk), 0)
    c = lax.broadcasted_iota(jnp.int32, (tq, nk), 1)
    dw = r - c + WINDOW
    valid = (qpos0 - WINDOW + c >= 0) & (dw >= 0) & (dw < WINDOW)
    g3 = g3_ref[0]
    outs = []
    for g in range(NSA_GROUPS):
        kt_g = kvt[g * NSA_DK:(g + 1) * NSA_DK, :].astype(BF16)
        vt_g = kvt[GD + g * NSA_DK:GD + (g + 1) * NSA_DK, :].astype(BF16)
        for hl in range(NSA_HPG):
            h = g * NSA_HPG + hl
            q_h = q_ref[0, :, h * NSA_DK:(h + 1) * NSA_DK]
            s = jnp.dot(q_h, kt_g, preferred_element_type=F32) + bias_ref[h]
            s = jnp.where(valid, s, NEG_INF)
            m = jnp.max(s, axis=-1, keepdims=True)
            p = jnp.where(valid, jnp.exp(s - m), 0.0)
            p = p / jnp.maximum(jnp.sum(p, axis=-1, keepdims=True), TINY)
            outs.append(_dot_t(p.astype(BF16), vt_g) * _gate_col(g3, h, 2))
    o_ref[0] = jnp.concatenate(outs, axis=1)


def _win_attn(q, g3, bias, pieces, *, t0, tq):
    nb, tt, _ = q.shape
    tile = lambda width: pl.BlockSpec((1, tq, width), lambda b, i: (b, i, 0))
    in_specs = [tile(D_MODEL), tile(LANE),
                pl.BlockSpec(bias.shape, lambda b, i: (0, 0, 0), pipeline_mode=pl.Buffered(1))]
    for arr, npos, idx in pieces:
        in_specs.append(pl.BlockSpec((1, 2 * GD, npos), functools.partial(lambda idx, b, i: (b, 0, idx(i)), idx)))
    return pl.pallas_call(
        functools.partial(_win_kernel, t0, tq, len(pieces)),
        grid=(nb, tt // tq),
        in_specs=in_specs,
        out_specs=tile(D_MODEL),
        out_shape=jax.ShapeDtypeStruct((nb, tt, D_MODEL), F32),
        compiler_params=_cparams(("parallel", "arbitrary")),
        name="win_attn",
    )(q, g3, bias, *[a for a, _, _ in pieces])


def _window_bias(tab_nsa, tq, nk):
    r = np.arange(tq)[:, None]
    c = np.arange(nk)[None, :]
    return jnp.transpose(tab_nsa[_t5_bucket_np(r - c + WINDOW)], (2, 0, 1))


def _group_rows_bias(bias_h):
    h, n, tq, tk = bias_h.shape
    b = bias_h.reshape(NSA_GROUPS, NSA_HPG, n, tq, tk)
    return jnp.transpose(b, (0, 2, 1, 3, 4)).reshape(NSA_GROUPS, n, NSA_HPG * tq, tk)


def _group_rows_far(tab_nsa, tq):
    far = tab_nsa[NUM_BUCKETS - 1].reshape(NSA_GROUPS, NSA_HPG, 1)
    return jnp.broadcast_to(far[:, :, None, :], (NSA_GROUPS, NSA_HPG, tq, 1)).reshape(NSA_GROUPS, NSA_HPG * tq, 1)


def _diff_kernel(t0, tq, ppt, has_past, lam_init, *refs):
    tab_ref = refs[0]
    refs = refs[2 if has_past else 1:]
    q_ref, lam_ref, sub_ref, far_ref, bn_ref, kvn_ref = refs[:6]
    refs = refs[6:]
    if has_past:
        bp_ref, kvp_refs = refs[0], refs[1:1 + ppt]
        refs = refs[1 + ppt:]
    o_ref, m_ref, l_ref, acc_ref = refs
    p = pl.program_id(1)
    i, bsel, kpos0 = tab_ref[T_I, p], tab_ref[T_BIAS, p], tab_ref[T_KPOS, p]
    _flash_prologue(tab_ref, p, m_ref, l_ref, acc_ref)
    first_map = lax.broadcasted_iota(jnp.int32, (tq, DIFF_DV), 1) < DIFF_DD

    def step(kv_refs, bias_ref):
        tk = sum(r.shape[1] for r in kv_refs)
        kpos = kpos0 + lax.broadcasted_iota(jnp.int32, (tq, tk), 1)
        qpos = t0 + i * tq + lax.broadcasted_iota(jnp.int32, (tq, tk), 0)
        valid = kpos <= qpos
        valid = jnp.concatenate([valid, valid], axis=0)
        for h in range(DIFF_HEADS):
            k_h = jnp.concatenate([r[0, :, 0, h, :] for r in kv_refs], axis=0).astype(BF16)
            v_h = jnp.concatenate([r[0, :, 1, h, :] for r in kv_refs], axis=0).astype(BF16)
            q_h = q_ref[0, :, h * DIFF_DV:(h + 1) * DIFF_DV]
            qq = jnp.concatenate([jnp.where(first_map, q_h, 0), jnp.where(first_map, 0, q_h)], axis=0).astype(BF16)
            bias = _tile_bias(bias_ref, h, bsel, far_ref[h])
            s = _dot_t(qq, k_h) + jnp.concatenate([bias, bias], axis=0)
            _online_softmax_step(s, valid, lambda pb: jnp.dot(pb, v_h, preferred_element_type=F32),
                                 m_ref.at[h], l_ref.at[h], acc_ref.at[h])

    if has_past:
        @pl.when(tab_ref[T_NEW, p] == 0)
        def _():
            step(list(kvp_refs), bp_ref)

        @pl.when(tab_ref[T_NEW, p] == 1)
        def _():
            step([kvn_ref], bn_ref)
    else:
        step([kvn_ref], bn_ref)

    @pl.when(tab_ref[T_LAST, p] == 1)
    def _():
        lam = lam_ref[0:1, 0:1]
        for h in range(DIFF_HEADS):
            o = acc_ref[h] / jnp.maximum(l_ref[h, :, 0:1], TINY)
            o = o[0:tq] - lam * o[tq:2 * tq]
            y = o * lax.rsqrt(jnp.mean(o * o, axis=-1, keepdims=True) + EPS) * sub_ref[...]
            o_ref[0, :, h * DIFF_DV:(h + 1) * DIFF_DV] = y * (1.0 - lam_init)


def _diff_attn(q, lam, subln, far, bias_new, new_rows, *, t0, tq, tkn, lam_init, past=None):
    nb, tt, _ = q.shape
    tkp = past[2] if past else tkn
    ppt = tkp // PAGE
    tab = _pair_tables(tt // tq, tq, t0, tkp, tkn)
    qtile, const = _flash_specs(tq)
    new_spec = pl.BlockSpec((1, tkn, 2, DIFF_HEADS, DIFF_DV),
                            lambda b, p, tab, *_: (b, tab[T_J, p] * tab[T_NEW, p], 0, 0, 0))
    in_specs = [qtile(D_MODEL), const(lam.shape), const(subln.shape), const(far.shape), const(bias_new.shape), new_spec]
    args = [q, lam, subln, far, bias_new, new_rows]
    prefetch = [tab]
    if past:
        page_spec = lambda k: pl.BlockSpec(
            (1, PAGE, 2, DIFF_HEADS, DIFF_DV), lambda b, p, tab, pt: (_past_page(tab, pt, b, p, ppt, k), 0, 0, 0, 0))
        in_specs += [const(past[3].shape)] + [page_spec(k) for k in range(ppt)]
        args += [past[3]] + [past[1]] * ppt
        prefetch.append(past[0])
    return pl.pallas_call(
        functools.partial(_diff_kernel, t0, tq, ppt, past is not None, lam_init),
        grid_spec=pltpu.PrefetchScalarGridSpec(
            num_scalar_prefetch=len(prefetch),
            grid=(nb, tab.shape[1]),
            in_specs=in_specs,
            out_specs=qtile(D_MODEL),
            scratch_shapes=[pltpu.VMEM((DIFF_HEADS, 2 * tq, LANE), F32), pltpu.VMEM((DIFF_HEADS, 2 * tq, LANE), F32),
                            pltpu.VMEM((DIFF_HEADS, 2 * tq, DIFF_DV), F32)]),
        out_shape=jax.ShapeDtypeStruct((nb, tt, D_MODEL), F32),
        compiler_params=_cparams(("parallel", "arbitrary")),
        name="diff_attn",
    )(*prefetch, *args)


def _out_mlp_kernel(x_ref, oc_ref, os_ref, ow_ref, od_ref, ga_ref, gb_ref, wo_ref, g_ref, wu_ref, wd_ref, y_ref):
    o_a = oc_ref[...] + os_ref[...] + ow_ref[...]
    o = ga_ref[...] * o_a + gb_ref[...] * od_ref[...]
    h = x_ref[...] + jnp.dot(o.astype(BF16), wo_ref[...], preferred_element_type=F32)
    hn = (h * lax.rsqrt(jnp.mean(h * h, axis=-1, keepdims=True) + EPS) * g_ref[...]).astype(BF16)
    y = h
    for c in range(D_FF // D_MODEL):
        cols = slice(c * D_MODEL, (c + 1) * D_MODEL)
        u = jnp.maximum(jnp.dot(hn, wu_ref[:, cols], preferred_element_type=F32), 0.0)
        y = y + jnp.dot((u * u).astype(BF16), wd_ref[cols, :], preferred_element_type=F32)
    y_ref[...] = y


def _out_mlp(x2d, oc, os_, ow, od, ga, gb, w_o, mlp_g, w_up, w_down, tm):
    m = x2d.shape[0]
    row = pl.BlockSpec((tm, D_MODEL), lambda i: (i, 0))
    const = lambda shape: pl.BlockSpec(shape, lambda i: (0, 0), pipeline_mode=pl.Buffered(1))
    return pl.pallas_call(
        _out_mlp_kernel,
        grid=(m // tm,),
        in_specs=[row] * 7 + [const(w_o.shape), const(mlp_g.shape), const(w_up.shape), const(w_down.shape)],
        out_specs=row,
        out_shape=jax.ShapeDtypeStruct((m, D_MODEL), F32),
        compiler_params=_cparams(("parallel",)),
        name="out_mlp",
    )(x2d, oc, os_, ow, od, ga, gb, w_o, mlp_g, w_up, w_down)


def _win_shift_kernel(n_new, past_ref, new_ref, o_ref):
    keep = o_ref.shape[2]
    o_ref[0] = jnp.concatenate([past_ref[0, :, n_new:keep], new_ref[0, :, 0:n_new]], axis=1)


def _win_shift(past_t, new_t, n_new):
    nb, feats, keep = past_t.shape
    return pl.pallas_call(
        functools.partial(_win_shift_kernel, n_new),
        grid=(nb,),
        in_specs=[pl.BlockSpec((1, feats, keep), lambda b: (b, 0, 0)),
                  pl.BlockSpec((1, feats, new_t.shape[2]), lambda b: (b, 0, 0))],
        out_specs=pl.BlockSpec((1, feats, keep), lambda b: (b, 0, 0)),
        out_shape=jax.ShapeDtypeStruct(past_t.shape, past_t.dtype),
        compiler_params=_cparams(("parallel",)),
        name="win_shift",
    )(past_t, new_t)


TQ_PROMPT = 256
TM_PROMPT = 256
TQ_SAMPLE = 8
TK_PAST = 512


def _round_up(x, m):
    return -(-x // m) * m


def _feature_major(cache):
    lead = cache.shape[:-4]
    p, s, g, dk = cache.shape[-4:]
    n = len(lead)
    return jnp.transpose(cache, tuple(range(n)) + (n + 1, n + 2, n + 3, n)).reshape(lead + (s * g * dk, p))


def _position_major(feat, slots):
    lead = feat.shape[:-2]
    p = feat.shape[-1]
    n = len(lead)
    x = feat.reshape(lead + (slots, NSA_GROUPS, NSA_DK, p))
    return jnp.transpose(x, tuple(range(n)) + (n + 3, n, n + 1, n + 2))


def _layer_params(l, lam_init, rel_bias_table, attn_norm, w_in, nsa_q_gain, nsa_k_gain,
                  cmp_pe_k, cmp_w1_k, cmp_w2_k, cmp_pe_v, cmp_w1_v, cmp_w2_v,
                  diff_q_gain, diff_k_gain, lambda_q1, lambda_k1, lambda_q2, lambda_k2, diff_subln,
                  w_o, mlp_norm, w_up, w_down):
    tile4 = lambda g: jnp.tile(g.astype(F32), MXU_DIM // NSA_DK)
    gains = jnp.zeros((8, MXU_DIM), F32)
    for r, g in enumerate((nsa_q_gain[l], diff_q_gain[l], diff_k_gain[l])):
        gains = gains.at[r].set(tile4(g))
    flat_pe = lambda pe: jnp.broadcast_to(pe.reshape(1, CMP_BLOCK * NSA_DK), (8, CMP_BLOCK * NSA_DK))
    w1 = jnp.stack([cmp_w1_k[l], cmp_w1_v[l]])
    w2 = jnp.stack([cmp_w2_k[l], cmp_w2_v[l]])
    w2z = jnp.zeros((2, NSA_GROUPS, CMP_HID, GD), F32)
    for g in range(NSA_GROUPS):
        w2z = w2z.at[:, g, :, g * NSA_DK:(g + 1) * NSA_DK].set(w2)
    lam = (jnp.exp(jnp.sum(lambda_q1[l].astype(F32) * lambda_k1[l].astype(F32)))
           - jnp.exp(jnp.sum(lambda_q2[l].astype(F32) * lambda_k2[l].astype(F32))) + lam_init)
    w_row, w_kvt = _split_w_in(w_in[l])
    k_gain = tile4(nsa_k_gain[l])
    return dict(
        attn_g=attn_norm[l].reshape(1, D_MODEL), w_row=w_row, w_kvt=w_kvt, bd=_segment_mean_matrix(),
        gains=gains, k_gain=k_gain.reshape(1, GD), k_gain_col=k_gain.reshape(GD, 1),
        pe2=jnp.stack([flat_pe(cmp_pe_k[l]), flat_pe(cmp_pe_v[l])]), w1=w1, w1_bf=w1.astype(BF16),
        w2z=w2z.astype(BF16),
        tab_nsa=rel_bias_table[:, :NSA_HEADS].astype(F32), tab_diff=rel_bias_table[:, NSA_HEADS:].astype(F32),
        lam=jnp.broadcast_to(lam.reshape(1, 1), (8, LANE)), subln=diff_subln[l].reshape(1, DIFF_DV),
        w_o=w_o[l].astype(BF16), mlp_g=mlp_norm[l].reshape(1, D_MODEL),
        w_up=w_up[l].astype(BF16), w_down=w_down[l].astype(BF16))


def _run_proj(P, x, tm):
    kgain_t = jnp.broadcast_to(P['k_gain_col'], (GD, tm))
    return _proj(x, P['attn_g'], P['w_row'], P['w_kvt'], P['bd'], P['gains'], kgain_t, tm)


def _mixers(P, lam_init, q, g3, qd, nsa_t, win_pieces, diff_rows, cmp_pages, tail, *,
            t0, tq, tkn, seq_len, nsa_past, diff_past):
    nb = q.shape[0]
    total = t0 + seq_len
    h0 = _pe_terms(P['pe2'], P['w1'])
    kc, vc = _compress(cmp_pages[0], cmp_pages[1], nb, cmp_pages[2], tail, P['w1_bf'], h0, P['w2z'], P['bd'],
                       P['k_gain'], page_table=cmp_pages[3])
    ncp = kc.shape[1]
    nc = -(-total // CMP_STRIDE) - CMP_BLOCK // CMP_STRIDE + 1
    n_slc = -(-total // SEL_BLOCK)
    nsp = _round_up(n_slc, LANE)
    bwin, far_c, selw = _cmp_tables(P['tab_nsa'], tq, ncp, n_slc, nsp)
    o_cmp, msk = _cmp_topk(q, kc, vc, g3, bwin, far_c, selw, t0=t0, tq=tq, nc=nc, n_slc=n_slc,
                           n_top=min(N_SEL, n_slc))
    far_s = _group_rows_far(P['tab_nsa'], tq)
    sel_past = dpast = None
    if nsa_past is not None:
        sel_past = (nsa_past[0], nsa_past[1], TK_PAST, _group_rows_bias(_toeplitz_bias(P['tab_nsa'], tq, TK_PAST)))
        dpast = (diff_past[0], diff_past[1], TK_PAST, _toeplitz_bias(P['tab_diff'], tq, TK_PAST))
    o_sel = _sel_attn(q, msk, g3, far_s, _group_rows_bias(_toeplitz_bias(P['tab_nsa'], tq, tkn)), nsa_t,
                      t0=t0, tq=tq, tkn=tkn, past=sel_past)
    nk = sum(n for _, n, _ in win_pieces)
    o_win = _win_attn(q, g3, _window_bias(P['tab_nsa'], tq, nk), win_pieces, t0=t0, tq=tq)
    far_d = P['tab_diff'][NUM_BUCKETS - 1].reshape(DIFF_HEADS, 1, 1)
    tkd = diff_rows.shape[1] if diff_past is not None else tkn
    o_diff = _diff_attn(qd, P['lam'], P['subln'], far_d, _toeplitz_bias(P['tab_diff'], tq, tkd), diff_rows,
                        t0=t0, tq=tq, tkn=tkd, lam_init=lam_init, past=dpast)
    return o_cmp, o_sel, o_win, o_diff


def _prompt_layer(P, lam_init, x):
    b, t, _ = x.shape
    tq = TQ_PROMPT
    q, nsa_t, win_t, g3, qd, diff5, ga, gb = _run_proj(P, x, TM_PROMPT)
    back = WINDOW // tq
    pieces = [(win_t, tq, functools.partial(lambda k, i: jnp.maximum(i - k, 0), k)) for k in range(back, -1, -1)]
    tail = jnp.zeros((b, 2 * GD, PAGE), F32)
    cmp_pages = (nsa_t, lambda bb, page, pt: (bb, 0, page), t // PAGE, None)
    o_cmp, o_sel, o_win, o_diff = _mixers(
        P, lam_init, q, g3, qd, nsa_t, pieces, diff5, cmp_pages, tail,
        t0=0, tq=tq, tkn=tq, seq_len=t, nsa_past=None, diff_past=None)
    f2 = lambda a: a.reshape(b * t, D_MODEL)
    y = _out_mlp(f2(x), f2(o_cmp), f2(o_sel), f2(o_win), f2(o_diff), f2(ga), f2(gb),
                 P['w_o'], P['mlp_g'], P['w_up'], P['w_down'], TM_PROMPT)
    keep = min(WINDOW, t)
    return y.reshape(b, t, D_MODEL), _position_major(nsa_t, 4), diff5, _position_major(win_t[:, :, t - keep:], 2)


def _sample_layer(P, lam_init, x, cache_nsa, cache_diff, cache_win, page_table):
    nb, t, _ = x.shape
    tq = TQ_SAMPLE
    assert t <= tq and cache_win.shape[1] == WINDOW
    past_len = page_table.shape[1] * PAGE
    rows = nb * t
    q, nsa_t, win_t, g3, qd, diff5, ga, gb = _run_proj(P, x.reshape(1, rows, D_MODEL), rows)
    per_seq = lambda a: jnp.pad(a.reshape(nb, t, a.shape[-1]), ((0, 0), (0, tq - t), (0, 0)))
    per_seq_t = lambda a: jnp.pad(jnp.transpose(a.reshape(a.shape[1], nb, t), (1, 0, 2)),
                                  ((0, 0), (0, 0), (0, PAGE - t)))
    nsa_new_t, win_new_t = per_seq_t(nsa_t), per_seq_t(win_t)
    diff_new = jnp.pad(diff5.reshape(nb, t, 2, DIFF_HEADS, DIFF_DV), ((0, 0), (0, tq - t), (0, 0), (0, 0), (0, 0)))
    pages_t = _feature_major(cache_nsa)
    past_win_t = _feature_major(cache_win)
    pieces = [(past_win_t, WINDOW, lambda i: 0), (win_new_t, PAGE, lambda i: 0)]
    cmp_pages = (pages_t, lambda bb, page, pt: (pt[bb, page], 0, 0), page_table.shape[1], page_table)
    o_cmp, o_sel, o_win, o_diff = _mixers(
        P, lam_init, per_seq(q[0]), per_seq(g3[0]), per_seq(qd[0]), nsa_new_t, pieces, diff_new, cmp_pages, nsa_new_t,
        t0=past_len, tq=tq, tkn=PAGE, seq_len=t, nsa_past=(page_table, pages_t), diff_past=(page_table, cache_diff))
    f2 = lambda a: a[:, :t].reshape(rows, D_MODEL)
    y = _out_mlp(x.reshape(rows, D_MODEL), f2(o_cmp), f2(o_sel), f2(o_win), f2(o_diff), ga[0], gb[0],
                 P['w_o'], P['mlp_g'], P['w_up'], P['w_down'], rows)
    win_out_t = _win_shift(past_win_t, win_new_t, t)
    nsa_rows = _position_major(nsa_new_t[:, :, :t], 4)
    return (y.reshape(nb, t, D_MODEL), nsa_rows, diff5.reshape(nb, t, 2, DIFF_HEADS, DIFF_DV),
            _position_major(win_out_t, 2))


def kernel(x_prompt, x_sample, cache_nsa_kv, cache_diff_kv, cache_nsa_win, page_table, rel_bias_table, attn_norm, w_in, nsa_q_gain, nsa_k_gain, cmp_pe_k, cmp_w1_k, cmp_w2_k, cmp_pe_v, cmp_w1_v, cmp_w2_v, diff_q_gain, diff_k_gain, lambda_q1, lambda_k1, lambda_q2, lambda_k2, diff_subln, w_o, mlp_norm, w_up, w_down):
    depth = w_in.shape[0]
    yp, ys = x_prompt, x_sample
    outs = [[] for _ in range(6)]
    for l in range(depth):
        lam_init = 0.8 - 0.6 * math.exp(-0.3 * l)
        P = _layer_params(l, lam_init, rel_bias_table, attn_norm, w_in, nsa_q_gain, nsa_k_gain,
                          cmp_pe_k, cmp_w1_k, cmp_w2_k, cmp_pe_v, cmp_w1_v, cmp_w2_v,
                          diff_q_gain, diff_k_gain, lambda_q1, lambda_k1, lambda_q2, lambda_k2, diff_subln,
                          w_o, mlp_norm, w_up, w_down)
        yp, nsa_p, diff_p, win_p = _prompt_layer(P, lam_init, yp)
        ys, nsa_s, diff_s, win_s = _sample_layer(P, lam_init, ys, cache_nsa_kv[l], cache_diff_kv[l],
                                                 cache_nsa_win[l], page_table)
        for dst, v in zip(outs, (nsa_p, diff_p, win_p, nsa_s, diff_s, win_s)):
            dst.append(v)
    return (yp, ys) + tuple(jnp.stack(o, 0) for o in outs)
```

```python
import functools
import math

import numpy as np
import jax
import jax.numpy as jnp
from jax import lax
from jax.experimental import pallas as pl
from jax.experimental.pallas import tpu as pltpu

F32 = jnp.float32
BF16 = jnp.bfloat16

D_MODEL = 1024
NSA_HEADS = 16
NSA_GROUPS = 4
NSA_HPG = NSA_HEADS // NSA_GROUPS
NSA_DK = D_MODEL // NSA_HEADS
GD = NSA_GROUPS * NSA_DK
CMP_BLOCK = 32
CMP_STRIDE = 16
CMP_HID = 4 * NSA_DK
SEL_BLOCK = 64
N_SEL = 16
WINDOW = 512
FORCE_BONUS = 1e4
DIFF_HEADS = 8
DIFF_DD = D_MODEL // (2 * DIFF_HEADS)
DIFF_DV = 2 * DIFF_DD
D_FF = 4 * D_MODEL
NUM_BUCKETS = 32
MAX_DISTANCE = 128
EPS = 1e-6
NEG_INF = -1e30
TINY = 1e-30
QK_SCALE = NSA_DK ** -0.5

LANE = 128
MXU_DIM = 256
VMEM_LIMIT = 56 * 1024 * 1024
PAGE = 128

C_QN = 0
C_QD = C_QN + D_MODEL
C_KD = C_QD + D_MODEL
C_VD = C_KD + D_MODEL
C_GA = C_VD + D_MODEL
C_GB = C_GA + D_MODEL
C_G3 = C_GB + D_MODEL
N_ROWMAJOR = C_G3 + LANE


def _cparams(sem):
    return pltpu.CompilerParams(dimension_semantics=sem, vmem_limit_bytes=VMEM_LIMIT)


def _t5_bucket_np(dist):
    n = np.maximum(dist, 0)
    max_exact = NUM_BUCKETS // 2
    nf = np.maximum(n, 1).astype(np.float32)
    log_ratio = np.log(nf / np.float32(max_exact)) / np.float32(math.log(MAX_DISTANCE / max_exact))
    large = max_exact + (log_ratio * np.float32(NUM_BUCKETS - max_exact)).astype(np.int32)
    large = np.minimum(large, NUM_BUCKETS - 1)
    return np.where(n < max_exact, n, large).astype(np.int32)


def _segment_mean_matrix():
    i = np.arange(MXU_DIM)
    return jnp.asarray((i[:, None] // NSA_DK == i[None, :] // NSA_DK) / NSA_DK, dtype=BF16)


def _dot_t(a, b, precision=None):
    return lax.dot_general(a, b, (((1,), (1,)), ((), ())), preferred_element_type=F32, precision=precision)


def _bias_lookup_kernel(hp, tab_ref, idx_ref, o_ref):
    n, r, c = idx_ref.shape
    for k in range(n):
        idx = idx_ref[k]
        for hh in range(hp):
            h = pl.program_id(0) * hp + hh
            acc = jnp.zeros((r, c), F32)
            for b in range(NUM_BUCKETS):
                acc = jnp.where(idx == b, tab_ref[h, b], acc)
            o_ref[0, k, hh * r:(hh + 1) * r, :] = acc


def _bias_lookup(tab_t, idx, hp=1):
    heads = tab_t.shape[0]
    n, r, c = idx.shape
    return pl.pallas_call(
        functools.partial(_bias_lookup_kernel, hp),
        grid=(heads // hp,),
        in_specs=[pl.BlockSpec(memory_space=pltpu.SMEM),
                  pl.BlockSpec((n, r, c), lambda g: (0, 0, 0), pipeline_mode=pl.Buffered(1))],
        out_specs=pl.BlockSpec((1, n, hp * r, c), lambda g: (g, 0, 0, 0)),
        out_shape=jax.ShapeDtypeStruct((heads // hp, n, hp * r, c), F32),
        compiler_params=_cparams(("parallel",)),
        name="bias_lookup",
    )(tab_t, jnp.asarray(idx, dtype=jnp.int32))


def _proj_kernel(x_ref, g_ref, w_ref, wt_ref, bd_ref, gains_ref, kgt_ref,
                 q_ref, nsat_ref, wint_ref, g3_ref, qd_ref, diff_ref, ga_ref, gb_ref, kvd_ref, selt_ref):
    x = x_ref[0]
    xn = x * lax.rsqrt(jnp.mean(x * x, axis=-1, keepdims=True) + EPS) * g_ref[...]
    xn = xn.astype(BF16)
    bd = bd_ref[...]

    def mm(c0, width=MXU_DIM):
        return jnp.dot(xn, w_ref[:, c0:c0 + width], preferred_element_type=F32)

    def headnorm(acc, gain_row):
        ms = jnp.dot((acc * acc).astype(BF16), bd, preferred_element_type=F32)
        return acc * lax.rsqrt(ms + EPS) * gains_ref[gain_row:gain_row + 1, :]

    heads_per_tile = MXU_DIM // DIFF_DV
    for c in range(D_MODEL // MXU_DIM):
        o = c * MXU_DIM
        q_ref[0, :, o:o + MXU_DIM] = (headnorm(mm(C_QN + o), 0) * QK_SCALE).astype(BF16)
        qd_ref[0, :, o:o + MXU_DIM] = (headnorm(mm(C_QD + o), 1) * QK_SCALE).astype(BF16)
        kd = headnorm(mm(C_KD + o), 2)
        vd = mm(C_VD + o)
        for hh in range(heads_per_tile):
            h = c * heads_per_tile + hh
            diff_ref[0, :, 0, h, :] = kd[:, hh * DIFF_DV:(hh + 1) * DIFF_DV]
            diff_ref[0, :, 1, h, :] = vd[:, hh * DIFF_DV:(hh + 1) * DIFF_DV]
        kvd_ref[0, :, o:o + MXU_DIM] = kd.astype(BF16)
        kvd_ref[0, :, D_MODEL + o:D_MODEL + o + MXU_DIM] = vd.astype(BF16)
        ga_ref[0, :, o:o + MXU_DIM] = jax.nn.sigmoid(mm(C_GA + o))
        gb_ref[0, :, o:o + MXU_DIM] = jax.nn.sigmoid(mm(C_GB + o))
    g3_ref[0] = jax.nn.sigmoid(mm(C_G3, LANE))

    def mm_t(slot):
        return _dot_t(wt_ref[slot * GD:(slot + 1) * GD, :], xn)

    def headnorm_t(acc):
        ms = jnp.dot(bd, (acc * acc).astype(BF16), preferred_element_type=F32)
        return acc * lax.rsqrt(ms + EPS) * kgt_ref[...]

    nsat_ref[0, 0:GD, :] = mm_t(0)
    nsat_ref[0, GD:2 * GD, :] = mm_t(1)
    k_sel, v_sel = headnorm_t(mm_t(2)), mm_t(3)
    nsat_ref[0, 2 * GD:3 * GD, :] = k_sel
    nsat_ref[0, 3 * GD:4 * GD, :] = v_sel
    selt_ref[0, 0:GD, :] = k_sel.astype(BF16)
    selt_ref[0, GD:2 * GD, :] = v_sel.astype(BF16)
    wint_ref[0, 0:GD, :] = headnorm_t(mm_t(4))
    wint_ref[0, GD:2 * GD, :] = mm_t(5)


def _proj(x, attn_g, w_row, w_kvt, bd, gains, kgain_t, tm):
    nb, t, _ = x.shape
    row = lambda width: pl.BlockSpec((1, tm, width), lambda b, i: (b, i, 0))
    col = lambda feats: pl.BlockSpec((1, feats, tm), lambda b, i: (b, 0, i))
    const = lambda shape: pl.BlockSpec(shape, lambda b, i: (0, 0), pipeline_mode=pl.Buffered(1))
    out_specs = (row(D_MODEL), col(4 * GD), col(2 * GD), row(LANE), row(D_MODEL),
                 pl.BlockSpec((1, tm, 2, DIFF_HEADS, DIFF_DV), lambda b, i: (b, i, 0, 0, 0)),
                 row(D_MODEL), row(D_MODEL), row(2 * D_MODEL), col(2 * GD))
    out_shapes = (
        jax.ShapeDtypeStruct((nb, t, D_MODEL), BF16),
        jax.ShapeDtypeStruct((nb, 4 * GD, t), F32),
        jax.ShapeDtypeStruct((nb, 2 * GD, t), F32),
        jax.ShapeDtypeStruct((nb, t, LANE), F32),
        jax.ShapeDtypeStruct((nb, t, D_MODEL), BF16),
        jax.ShapeDtypeStruct((nb, t, 2, DIFF_HEADS, DIFF_DV), F32),
        jax.ShapeDtypeStruct((nb, t, D_MODEL), F32),
        jax.ShapeDtypeStruct((nb, t, D_MODEL), F32),
        jax.ShapeDtypeStruct((nb, t, 2 * D_MODEL), BF16),
        jax.ShapeDtypeStruct((nb, 2 * GD, t), BF16),
    )
    return pl.pallas_call(
        _proj_kernel,
        grid=(nb, t // tm),
        in_specs=[row(D_MODEL), const((1, D_MODEL)), const(w_row.shape), const(w_kvt.shape),
                  const(bd.shape), const(gains.shape), const(kgain_t.shape)],
        out_specs=out_specs,
        out_shape=out_shapes,
        compiler_params=_cparams(("parallel", "parallel")),
        name="proj",
    )(x, attn_g, w_row, w_kvt, bd, gains, kgain_t)


def _split_w_in(w_in):
    sizes = (D_MODEL, 6 * GD, 3 * NSA_HEADS, D_MODEL, D_MODEL, D_MODEL, D_MODEL, D_MODEL)
    pts = np.cumsum(sizes)[:-1]
    q_n, kv_n, g_n, q_d, k_d, v_d, ga, gb = jnp.split(w_in, pts, axis=-1)
    g_pad = jnp.pad(g_n, ((0, 0), (0, LANE - 3 * NSA_HEADS)))
    w_row = jnp.concatenate([q_n, q_d, k_d, v_d, ga, gb, g_pad], axis=-1).astype(BF16)
    return w_row, kv_n.T.astype(BF16)


CH_PER_PAGE = PAGE // CMP_STRIDE
J_PER_TILE = MXU_DIM // NSA_DK


def _pe_kernel(pe_ref, w1_ref, o_ref):
    for kv in range(2):
        o_ref[kv] = jnp.dot(pe_ref[kv], w1_ref[kv], preferred_element_type=F32,
                            precision=lax.Precision.HIGHEST)


def _pe_terms(pe2, w1):
    return pl.pallas_call(
        _pe_kernel,
        out_shape=jax.ShapeDtypeStruct((2, 8, CMP_HID), F32),
        name="cmp_pe",
    )(pe2, w1)


def _compress_kernel(pps, *refs):
    page_refs = refs[1:1 + pps]
    next_ref, tail_ref, w1_ref, h0_ref, w2_ref, bd_ref, kg_ref, kc_ref, vc_ref = refs[1 + pps:10 + pps]
    rows_refs = refs[10 + pps:]
    is_last = pl.program_id(1) == pl.num_programs(1) - 1
    cb = pps * CH_PER_PAGE
    rows = cb + CH_PER_PAGE
    gpl = LANE // NSA_DK
    for fb, rows_ref in enumerate(rows_refs):
        feats = slice(fb * LANE, (fb + 1) * LANE)
        for k, p in enumerate(page_refs):
            rows_ref[k * PAGE:(k + 1) * PAGE, :] = p[0, feats, :].T
        rows_ref[pps * PAGE:(pps + 1) * PAGE, :] = jnp.where(is_last, tail_ref[0, feats, :], next_ref[0, feats, :]).T

    for kv in range(2):
        acc = [None, None]
        for jt in range(CMP_STRIDE // J_PER_TILE):
            xs = [[rows_refs[kv * (GD // LANE) + half][pl.ds(jt * J_PER_TILE + jj, rows, stride=CMP_STRIDE), :]
                   for half in range(GD // LANE)] for jj in range(J_PER_TILE)]
            lhs = jnp.concatenate(
                [jnp.concatenate([x[g // gpl][:, (g % gpl) * NSA_DK:(g % gpl + 1) * NSA_DK] for x in xs], axis=1)
                 for g in range(NSA_GROUPS)], axis=0).astype(BF16)
            for r in range(2):
                k0 = r * CMP_STRIDE * NSA_DK + jt * MXU_DIM
                part = jnp.dot(lhs, w1_ref[kv, k0:k0 + MXU_DIM, :], preferred_element_type=F32)
                acc[r] = part if acc[r] is None else acc[r] + part
        out = None
        for g in range(NSA_GROUPS):
            h = h0_ref[kv, 0:1, :] + acc[0][g * rows:g * rows + cb] + acc[1][g * rows + 1:g * rows + 1 + cb]
            part = jnp.dot(jax.nn.gelu(h).astype(BF16), w2_ref[kv, g], preferred_element_type=F32)
            out = part if out is None else out + part
        if kv == 0:
            ms = jnp.dot((out * out).astype(BF16), bd_ref[...], preferred_element_type=F32)
            kc_ref[0] = out * lax.rsqrt(ms + EPS) * kg_ref[...]
        else:
            vc_ref[0] = out


def _compress(pages, page_index, nb, npg, tail, w1, h0, w2z, bd, kgain, page_table=None):
    pps = min(8, npg)
    assert npg % pps == 0
    prefetch = [page_table if page_table is not None else jnp.zeros((1, 1), jnp.int32)]
    blk = (1, 2 * GD, PAGE)
    page_spec = lambda k: pl.BlockSpec(blk, lambda b, s, pt: page_index(b, s * pps + k, pt))
    next_spec = pl.BlockSpec(blk, lambda b, s, pt: page_index(b, jnp.minimum(s * pps + pps, npg - 1), pt))
    const = lambda shape: pl.BlockSpec(shape, lambda b, s, pt: (0,) * len(shape), pipeline_mode=pl.Buffered(1))
    out_spec = pl.BlockSpec((1, pps * CH_PER_PAGE, GD), lambda b, s, pt: (b, s, 0))
    out_sds = jax.ShapeDtypeStruct((nb, npg * CH_PER_PAGE, GD), F32)
    return pl.pallas_call(
        functools.partial(_compress_kernel, pps),
        grid_spec=pltpu.PrefetchScalarGridSpec(
            num_scalar_prefetch=1,
            grid=(nb, npg // pps),
            in_specs=[page_spec(k) for k in range(pps)] + [
                next_spec,
                pl.BlockSpec(blk, lambda b, s, pt: (b, 0, 0)),
                const(w1.shape), const(h0.shape), const(w2z.shape), const(bd.shape), const(kgain.shape)],
            out_specs=(out_spec, out_spec),
            scratch_shapes=[pltpu.VMEM(((pps + 1) * PAGE, LANE), F32)] * (2 * GD // LANE)),
        out_shape=(out_sds, out_sds),
        compiler_params=_cparams(("parallel", "arbitrary")),
        name="compress",
    )(*prefetch, *([pages] * (pps + 1)), tail, w1, h0, w2z, bd, kgain)


CMP_WIN = 64
CMP_WIN_BACK = 32
SCORE_PAD = -3.0e38


def _gate_col(g3, head, branch):
    c = head * 3 + branch
    return g3[:, c:c + 1]


def _cmp_topk_kernel(t0, tq, ncp, nc, n_slc, nsp, n_top,
                     q_ref, kc_ref, vc_ref, g3_ref, bias_ref, selwt_ref, o_ref, msk_ref):
    qpos0 = t0 + pl.program_id(1) * tq
    col = lax.broadcasted_iota(jnp.int32, (tq, ncp), 1)
    qpos = qpos0 + lax.broadcasted_iota(jnp.int32, (tq, ncp), 0)
    valid_c = (col * CMP_STRIDE + (CMP_BLOCK - 1) <= qpos) & (col < nc)
    shift = (qpos0 // CMP_STRIDE - CMP_WIN_BACK + ncp) % ncp
    g3 = g3_ref[0]
    tqp = _round_up(tq, LANE)
    blk = lax.broadcasted_iota(jnp.int32, (nsp, tqp), 0)
    bpos = qpos0 + lax.broadcasted_iota(jnp.int32, (nsp, tqp), 1)
    cur = bpos // SEL_BLOCK
    forced = (blk == 0) | (blk == cur) | (blk == cur - 1)
    valid_b = blk * SEL_BLOCK <= bpos
    outs, carry = [], []
    for g in range(NSA_GROUPS):
        kc_g = kc_ref[0, :, g * NSA_DK:(g + 1) * NSA_DK].astype(BF16)
        vc_g = vc_ref[0, :, g * NSA_DK:(g + 1) * NSA_DK].astype(BF16)
        psum = jnp.zeros((tq, ncp), F32)
        for hl in range(NSA_HPG):
            h = g * NSA_HPG + hl
            q_h = q_ref[0, :, h * NSA_DK:(h + 1) * NSA_DK]
            s = _dot_t(q_h, kc_g) + pltpu.roll(bias_ref[h, 0], shift, 1)
            s = jnp.where(valid_c, s, NEG_INF)
            m = jnp.max(s, axis=-1, keepdims=True)
            p = jnp.where(valid_c, jnp.exp(s - m), 0.0)
            p = p / jnp.maximum(jnp.sum(p, axis=-1, keepdims=True), TINY)
            o_h = jnp.dot(p.astype(BF16), vc_g, preferred_element_type=F32)
            outs.append(o_h * _gate_col(g3, h, 0))
            psum = psum + p
        if tqp != tq:
            psum = jnp.concatenate([psum, jnp.zeros((tqp - tq, ncp), F32)], axis=0)
        imp_t = _dot_t(selwt_ref[...], psum, precision=lax.Precision.HIGHEST)
        score = jnp.where(valid_b, imp_t + jnp.where(forced, FORCE_BONUS, 0.0), -FORCE_BONUS)
        carry += [jnp.where(blk < n_slc, score, SCORE_PAD), jnp.zeros((nsp, tqp), F32)]

    def pick(_, carry):
        nxt = []
        for g in range(NSA_GROUPS):
            sc, chosen = carry[2 * g], carry[2 * g + 1]
            best = jnp.max(sc, axis=0, keepdims=True)
            first = jnp.min(jnp.where(sc == best, blk, nsp), axis=0, keepdims=True)
            hit = blk == first
            nxt += [jnp.where(hit, SCORE_PAD, sc), jnp.where(hit, 1.0, chosen)]
        return tuple(nxt)

    carry = lax.fori_loop(0, n_top, pick, tuple(carry))
    for g in range(NSA_GROUPS):
        msk_ref[0, :, g * nsp:(g + 1) * nsp] = ((1.0 - carry[2 * g + 1].T[0:tq]) * MASK_NEG).astype(BF16)
    o_ref[0] = jnp.concatenate(outs, axis=1)


def _cmp_topk(q, kc, vc, g3, bias, selwt, *, t0, tq, nc, n_slc, n_top):
    nb, tt, _ = q.shape
    ncp = kc.shape[1]
    nsp = selwt.shape[0]
    assert CMP_STRIDE * ncp > tq + CMP_STRIDE * CMP_WIN_BACK - CMP_BLOCK, (ncp, tq)
    const = lambda shape: pl.BlockSpec(shape, lambda b, i: (0,) * len(shape), pipeline_mode=pl.Buffered(1))
    per_b = lambda shape: pl.BlockSpec(shape, lambda b, i: (b, 0, 0))
    tile = lambda width: pl.BlockSpec((1, tq, width), lambda b, i: (b, i, 0))
    return pl.pallas_call(
        functools.partial(_cmp_topk_kernel, t0, tq, ncp, nc, n_slc, nsp, n_top),
        grid=(nb, tt // tq),
        in_specs=[tile(D_MODEL), per_b((1, ncp, GD)), per_b((1, ncp, GD)), tile(LANE),
                  const(bias.shape), const(selwt.shape)],
        out_specs=(tile(D_MODEL), tile(NSA_GROUPS * nsp)),
        out_shape=(jax.ShapeDtypeStruct((nb, tt, D_MODEL), F32),
                   jax.ShapeDtypeStruct((nb, tt, NSA_GROUPS * nsp), BF16)),
        compiler_params=_cparams(("parallel", "arbitrary")),
        name="cmp_topk",
    )(q, kc, vc, g3, bias, selwt)


def _cmp_tables(tab_nsa_t, tq, ncp, n_slc, nsp):
    r = np.arange(tq)[:, None]
    w = np.arange(ncp)[None, :]
    dc = r - CMP_STRIDE * (w - CMP_WIN_BACK) - (CMP_BLOCK - 1)
    idx = np.where((w < CMP_WIN) & (dc >= 0), _t5_bucket_np(dc), NUM_BUCKETS - 1)
    bias = _bias_lookup(tab_nsa_t, idx[None])
    c_lo = np.arange(ncp)[None, :] * CMP_STRIDE
    s_lo = np.arange(nsp)[:, None] * SEL_BLOCK
    shared = np.clip(np.minimum(c_lo + CMP_BLOCK, s_lo + SEL_BLOCK) - np.maximum(c_lo, s_lo), 0, None)
    shared = np.where(np.arange(nsp)[:, None] < n_slc, shared, 0)
    return bias, jnp.asarray(shared / CMP_BLOCK, dtype=F32)


BIAS_DIAG, BIAS_PREV, BIAS_FAR = 0, 1, 2
T_I, T_J, T_FIRST, T_LAST, T_BIAS, T_NEW, T_KPOS = range(7)


def _pair_tables(nq, tq, t0, tkp, tkn):
    assert t0 % tkp == 0
    n_past = t0 // tkp
    rows = []
    for i in range(nq):
        q0 = t0 + i * tq
        tiles = [(0, j, j * tkp, tkp) for j in range(n_past)]
        tiles += [(1, j, t0 + j * tkn, tkn) for j in range((i * tq + tq - 1) // tkn + 1)]
        for n, (is_new, j, k0, tk) in enumerate(tiles):
            delta = q0 - k0
            assert delta in (0, tk) or delta >= tk + MAX_DISTANCE - 1, (tq, t0, tkp, tkn)
            bsel = BIAS_DIAG if delta == 0 else BIAS_PREV if delta == tk else BIAS_FAR
            rows.append((i, j, int(n == 0), int(n == len(tiles) - 1), bsel, is_new, k0))
    tab = np.array(rows, dtype=np.int32).T.copy()
    cases = sorted({(int(r[T_NEW]), int(r[T_BIAS])) for r in tab.T})
    return jnp.asarray(tab), tuple(cases)


BAND = MAX_DISTANCE


def _band_bias(tab_t):
    r = np.arange(BAND)[:, None]
    c = np.arange(BAND)[None, :]
    tiles = _bias_lookup(tab_t, np.stack([_t5_bucket_np(r - c), _t5_bucket_np(r - c + BAND)]))
    return tiles - tab_t[:, NUM_BUCKETS - 1].reshape(-1, 1, 1, 1)


def _when_cases(tab_ref, p, cases, fn):
    for is_new, bsel in cases:
        pl.when((tab_ref[T_NEW, p] == is_new) & (tab_ref[T_BIAS, p] == bsel))(functools.partial(fn, is_new, bsel))


MASK_NEG = -1e30


def _tile_dtype(rows):
    return BF16 if rows % 16 == 0 else F32


def _band_correction(band_ref, head, tq, tk, delta):
    rb_rows, cb_cols = min(tq, BAND), min(tk, BAND)
    rows = []
    for rb in range(max(tq // BAND, 1)):
        cols = []
        for cb in range(max(tk // BAND, 1)):
            d = delta // BAND + rb - cb
            cols.append(band_ref[head, d, 0:rb_rows, 0:cb_cols] if d in (0, 1) else jnp.zeros((rb_rows, cb_cols), F32))
        rows.append(cols[0] if len(cols) == 1 else jnp.concatenate(cols, axis=1))
    return rows[0] if len(rows) == 1 else jnp.concatenate(rows, axis=0)


def _lane_tiles(x, n):
    return x if n == 1 else jnp.concatenate([x] * n, axis=1)


def _online_softmax_step(s, pv, m_ref, acc_ref):
    m_prev = m_ref[...]
    m_next = jnp.maximum(m_prev, jnp.max(s, axis=1)[:, None])
    p = jnp.exp(s - _lane_tiles(m_next, s.shape[1] // LANE))
    alpha = jnp.exp(m_prev - m_next)
    m_ref[...] = m_next
    acc_ref[...] = _lane_tiles(alpha, acc_ref.shape[1] // LANE) * acc_ref[...] + pv(p.astype(BF16))


def _unit_block(shape, axis):
    return jnp.where(lax.broadcasted_iota(jnp.int32, shape, axis) == 0, 1.0, 0.0).astype(BF16)


def _causal_neg(tq, tk):
    r = lax.broadcasted_iota(jnp.int32, (tq, tk), 0)
    c = lax.broadcasted_iota(jnp.int32, (tq, tk), 1)
    return jnp.where(c <= r, 0.0, MASK_NEG).astype(F32)


def _flash_prologue(tab_ref, p, m_ref, acc_ref):
    @pl.when(tab_ref[T_FIRST, p] == 1)
    def _():
        m_ref[...] = jnp.full(m_ref.shape, NEG_INF, F32)
        acc_ref[...] = jnp.zeros(acc_ref.shape, F32)


def _flash_specs(tq):
    qtile = lambda width: pl.BlockSpec((1, tq, width), lambda b, p, tab, *_: (b, tab[T_I, p], 0))
    const = lambda shape: pl.BlockSpec(shape, lambda b, p, tab, *_: (0,) * len(shape), pipeline_mode=pl.Buffered(1))
    return qtile, const


def _past_page(tab, pt, b, p, ppt, k):
    n_past = pt.shape[1] // ppt
    j = jnp.where(tab[T_NEW, p] == 1, n_past - 1, tab[T_J, p])
    return pt[b, j * ppt + k]


def _sel_kernel(t0, tq, ppt, has_past, nsp, cases, *refs):
    tab_ref = refs[0]
    refs = refs[2 if has_past else 1:]
    q_ref, msk_ref, g3_ref, band_ref, kvn_ref = refs[:5]
    refs = refs[5:]
    if has_past:
        kp_refs, vp_refs = refs[0:ppt], refs[ppt:2 * ppt]
        refs = refs[2 * ppt:]
    o_ref, m_ref, acc_ref, qh_ref = refs
    p = pl.program_id(1)
    kpos0 = tab_ref[T_KPOS, p]
    _flash_prologue(tab_ref, p, m_ref, acc_ref)
    stack = qh_ref.shape[1] // tq

    @pl.when(tab_ref[T_FIRST, p] == 1)
    def _():
        for hs in range(NSA_HEADS // stack):
            qh_ref[hs] = jnp.concatenate(
                [q_ref[0, :, h * NSA_DK:(h + 1) * NSA_DK] for h in range(hs * stack, (hs + 1) * stack)],
                axis=0).astype(qh_ref.dtype)

    def step(is_new, bsel):
        if is_new:
            kt_all, vt_all = kvn_ref[0, 0:GD, :], kvn_ref[0, GD:2 * GD, :]
        else:
            kt_all = jnp.concatenate([r[0] for r in kp_refs], axis=1)
            vt_all = jnp.concatenate([r[0] for r in vp_refs], axis=1)
        tk = kt_all.shape[1]
        blk_of_key = (kpos0 + lax.broadcasted_iota(jnp.int32, (nsp, tk), 1)) // SEL_BLOCK
        expand = (lax.broadcasted_iota(jnp.int32, (nsp, tk), 0) == blk_of_key).astype(BF16)
        for g in range(NSA_GROUPS):
            hidden = jnp.dot(msk_ref[0, :, g * nsp:(g + 1) * nsp], expand, preferred_element_type=F32)
            if bsel == BIAS_DIAG:
                hidden = hidden + _causal_neg(tq, tk)
            kt_g = kt_all[g * NSA_DK:(g + 1) * NSA_DK, :].astype(BF16)
            vt_g = jnp.concatenate([vt_all[g * NSA_DK:(g + 1) * NSA_DK, :].astype(BF16),
                                    _unit_block((LANE - NSA_DK, tk), 0)], axis=0)
            extra = hidden
            for hs in range(g * NSA_HPG // stack, (g + 1) * NSA_HPG // stack):
                if bsel != BIAS_FAR:
                    extra = [hidden + _band_correction(band_ref, h, tq, tk, 0 if bsel == BIAS_DIAG else tk)
                             for h in range(hs * stack, (hs + 1) * stack)]
                    extra = extra[0] if stack == 1 else jnp.concatenate(extra, axis=0)
                elif stack > 1:
                    extra = jnp.concatenate([hidden] * stack, axis=0)
                s = jnp.dot(qh_ref[hs].astype(BF16), kt_g, preferred_element_type=F32) + extra
                _online_softmax_step(s, lambda pb: _dot_t(pb, vt_g), m_ref.at[hs], acc_ref.at[hs])

    _when_cases(tab_ref, p, cases, step)

    @pl.when(tab_ref[T_LAST, p] == 1)
    def _():
        g3 = g3_ref[0]
        outs = []
        for h in range(NSA_HEADS):
            rows = slice((h % stack) * tq, (h % stack + 1) * tq)
            o = acc_ref[h // stack, rows, 0:NSA_DK] / jnp.maximum(acc_ref[h // stack, rows, NSA_DK:NSA_DK + 1], TINY)
            outs.append(o * _gate_col(g3, h, 1))
        o_ref[0] = jnp.concatenate(outs, axis=1)


def _sel_attn(q, msk, g3, band, new_t, *, t0, tq, tkn, past=None):
    nb, tt, _ = q.shape
    nsp = msk.shape[2] // NSA_GROUPS
    tkp = past[2] if past else tkn
    ppt = tkp // PAGE
    tab, cases = _pair_tables(tt // tq, tq, t0, tkp, tkn)
    stack = NSA_HPG if NSA_HPG * tq <= LANE else 1
    rows = stack * tq
    qtile, const = _flash_specs(tq)
    new_spec = pl.BlockSpec((1, 2 * GD, tkn), lambda b, p, tab, *_: (b, 0, tab[T_J, p] * tab[T_NEW, p]))
    in_specs = [qtile(D_MODEL), qtile(NSA_GROUPS * nsp), qtile(LANE), const(band.shape), new_spec]
    args = [q, msk, g3, band, new_t]
    prefetch = [tab]
    if past:
        page_spec = lambda slot, k: pl.BlockSpec(
            (1, GD, PAGE), lambda b, p, tab, pt: (_past_page(tab, pt, b, p, ppt, k), slot, 0))
        in_specs += [page_spec(s, k) for s in (2, 3) for k in range(ppt)]
        args += [past[1]] * (2 * ppt)
        prefetch.append(past[0])
    return pl.pallas_call(
        functools.partial(_sel_kernel, t0, tq, ppt, past is not None, nsp, cases),
        grid_spec=pltpu.PrefetchScalarGridSpec(
            num_scalar_prefetch=len(prefetch),
            grid=(nb, tab.shape[1]),
            in_specs=in_specs,
            out_specs=qtile(D_MODEL),
            scratch_shapes=[pltpu.VMEM((NSA_HEADS // stack, rows, LANE), F32),
                            pltpu.VMEM((NSA_HEADS // stack, rows, LANE), F32),
                            pltpu.VMEM((NSA_HEADS // stack, rows, NSA_DK), _tile_dtype(rows))]),
        out_shape=jax.ShapeDtypeStruct((nb, tt, D_MODEL), F32),
        compiler_params=_cparams(("parallel", "arbitrary")),
        name="sel_attn",
    )(*prefetch, *args)


def _win_kernel(t0, tq, n_piece, *refs):
    q_ref, g3_ref, bias_ref = refs[:3]
    piece_refs = refs[3:3 + n_piece]
    o_ref = refs[3 + n_piece]
    cols = [r[0] for r in piece_refs]
    kvt = cols[0] if n_piece == 1 else jnp.concatenate(cols, axis=1)
    nk = kvt.shape[1]
    qpos0 = t0 + pl.program_id(1) * tq
    r = lax.broadcasted_iota(jnp.int32, (tq, nk), 0)
    c = lax.broadcasted_iota(jnp.int32, (tq, nk), 1)
    dw = r - c + WINDOW
    valid = (qpos0 - WINDOW + c >= 0) & (dw >= 0) & (dw < WINDOW)
    g3 = g3_ref[0]
    outs = []
    for g in range(NSA_GROUPS):
        kt_g = kvt[g * NSA_DK:(g + 1) * NSA_DK, :].astype(BF16)
        vt_g = kvt[GD + g * NSA_DK:GD + (g + 1) * NSA_DK, :].astype(BF16)
        for hl in range(NSA_HPG):
            h = g * NSA_HPG + hl
            q_h = q_ref[0, :, h * NSA_DK:(h + 1) * NSA_DK]
            s = jnp.dot(q_h, kt_g, preferred_element_type=F32) + bias_ref[h]
            s = jnp.where(valid, s, NEG_INF)
            m = jnp.max(s, axis=-1, keepdims=True)
            p = jnp.where(valid, jnp.exp(s - m), 0.0)
            p = p / jnp.maximum(jnp.sum(p, axis=-1, keepdims=True), TINY)
            outs.append(_dot_t(p.astype(BF16), vt_g) * _gate_col(g3, h, 2))
    o_ref[0] = jnp.concatenate(outs, axis=1)


def _win_attn(q, g3, bias, pieces, *, t0, tq):
    nb, tt, _ = q.shape
    tile = lambda width: pl.BlockSpec((1, tq, width), lambda b, i: (b, i, 0))
    in_specs = [tile(D_MODEL), tile(LANE),
                pl.BlockSpec(bias.shape, lambda b, i: (0, 0, 0), pipeline_mode=pl.Buffered(1))]
    for arr, npos, idx in pieces:
        in_specs.append(pl.BlockSpec((1, 2 * GD, npos), functools.partial(lambda idx, b, i: (b, 0, idx(i)), idx)))
    return pl.pallas_call(
        functools.partial(_win_kernel, t0, tq, len(pieces)),
        grid=(nb, tt // tq),
        in_specs=in_specs,
        out_specs=tile(D_MODEL),
        out_shape=jax.ShapeDtypeStruct((nb, tt, D_MODEL), F32),
        compiler_params=_cparams(("parallel", "arbitrary")),
        name="win_attn",
    )(q, g3, bias, *[a for a, _, _ in pieces])


def _window_bias(tab_nsa_t, tq, nk):
    r = np.arange(tq)[:, None]
    c = np.arange(nk)[None, :]
    return _bias_lookup(tab_nsa_t, _t5_bucket_np(r - c + WINDOW)[None]).reshape(NSA_HEADS, tq, nk)


def _diff_kernel(t0, tq, ppt, has_past, lam_init, cases, *refs):
    tab_ref = refs[0]
    refs = refs[2 if has_past else 1:]
    q_ref, lam_ref, sub_ref, band_ref, kvn_ref = refs[:5]
    refs = refs[5:]
    if has_past:
        kvp_refs = refs[0:ppt]
        refs = refs[ppt:]
    o_ref, m_ref, acc_ref, qq_ref = refs
    p = pl.program_id(1)
    _flash_prologue(tab_ref, p, m_ref, acc_ref)

    @pl.when(tab_ref[T_FIRST, p] == 1)
    def _():
        first_map = lax.broadcasted_iota(jnp.int32, (tq, DIFF_DV), 1) < DIFF_DD
        for h in range(DIFF_HEADS):
            q_h = q_ref[0, :, h * DIFF_DV:(h + 1) * DIFF_DV].astype(F32)
            qq_ref[h] = jnp.concatenate([jnp.where(first_map, q_h, 0.0), jnp.where(first_map, 0.0, q_h)],
                                        axis=0).astype(qq_ref.dtype)

    def page_rows(first_row):
        return jnp.concatenate([r[0, pl.ds(first_row, PAGE, stride=2 * DIFF_HEADS), :] for r in kvp_refs],
                               axis=0).astype(BF16)

    def step(is_new, bsel):
        tk = kvn_ref.shape[1] if is_new else ppt * PAGE
        for h in range(DIFF_HEADS):
            if is_new:
                k_h = kvn_ref[0, :, h * DIFF_DV:(h + 1) * DIFF_DV]
                v_h = kvn_ref[0, :, D_MODEL + h * DIFF_DV:D_MODEL + (h + 1) * DIFF_DV]
            else:
                k_h, v_h = page_rows(h), page_rows(DIFF_HEADS + h)
            v_h = jnp.concatenate([v_h, _unit_block((tk, LANE), 1)], axis=1)
            s = _dot_t(qq_ref[h].astype(BF16), k_h)
            if bsel != BIAS_FAR:
                extra = _band_correction(band_ref, h, tq, tk, 0 if bsel == BIAS_DIAG else tk)
                if bsel == BIAS_DIAG:
                    extra = extra + _causal_neg(tq, tk)
                s = s + jnp.concatenate([extra, extra], axis=0)
            _online_softmax_step(s, lambda pb: jnp.dot(pb, v_h, preferred_element_type=F32),
                                 m_ref.at[h], acc_ref.at[h])

    _when_cases(tab_ref, p, cases, step)

    @pl.when(tab_ref[T_LAST, p] == 1)
    def _():
        lam = lam_ref[0:1, 0:1]
        for h in range(DIFF_HEADS):
            o = acc_ref[h, :, 0:DIFF_DV] / jnp.maximum(acc_ref[h, :, DIFF_DV:DIFF_DV + 1], TINY)
            o = o[0:tq] - lam * o[tq:2 * tq]
            y = o * lax.rsqrt(jnp.mean(o * o, axis=-1, keepdims=True) + EPS) * sub_ref[...]
            o_ref[0, :, h * DIFF_DV:(h + 1) * DIFF_DV] = y * (1.0 - lam_init)


def _diff_attn(q, lam, subln, band, new_rows, *, t0, tq, tkn, lam_init, past=None):
    nb, tt, _ = q.shape
    tkp = past[2] if past else tkn
    ppt = tkp // PAGE
    tab, cases = _pair_tables(tt // tq, tq, t0, tkp, tkn)
    qtile, const = _flash_specs(tq)
    new_spec = pl.BlockSpec((1, tkn, 2 * D_MODEL), lambda b, p, tab, *_: (b, tab[T_J, p] * tab[T_NEW, p], 0))
    in_specs = [qtile(D_MODEL), const(lam.shape), const(subln.shape), const(band.shape), new_spec]
    args = [q, lam, subln, band, new_rows]
    prefetch = [tab]
    if past:
        page_spec = lambda k: pl.BlockSpec(
            (1, PAGE * 2 * DIFF_HEADS, DIFF_DV), lambda b, p, tab, pt: (_past_page(tab, pt, b, p, ppt, k), 0, 0))
        in_specs += [page_spec(k) for k in range(ppt)]
        args += [past[1]] * ppt
        prefetch.append(past[0])
    return pl.pallas_call(
        functools.partial(_diff_kernel, t0, tq, ppt, past is not None, lam_init, cases),
        grid_spec=pltpu.PrefetchScalarGridSpec(
            num_scalar_prefetch=len(prefetch),
            grid=(nb, tab.shape[1]),
            in_specs=in_specs,
            out_specs=qtile(D_MODEL),
            scratch_shapes=[pltpu.VMEM((DIFF_HEADS, 2 * tq, LANE), F32),
                            pltpu.VMEM((DIFF_HEADS, 2 * tq, DIFF_DV + LANE), F32),
                            pltpu.VMEM((DIFF_HEADS, 2 * tq, DIFF_DV), BF16)]),
        out_shape=jax.ShapeDtypeStruct((nb, tt, D_MODEL), F32),
        compiler_params=_cparams(("parallel", "arbitrary")),
        name="diff_attn",
    )(*prefetch, *args)


def _out_mlp_kernel(x_ref, oc_ref, os_ref, ow_ref, od_ref, ga_ref, gb_ref, wo_ref, g_ref, wu_ref, wd_ref, y_ref):
    o_a = oc_ref[...] + os_ref[...] + ow_ref[...]
    o = ga_ref[...] * o_a + gb_ref[...] * od_ref[...]
    h = x_ref[...] + jnp.dot(o.astype(BF16), wo_ref[...], preferred_element_type=F32)
    hn = (h * lax.rsqrt(jnp.mean(h * h, axis=-1, keepdims=True) + EPS) * g_ref[...]).astype(BF16)
    y = h
    for c in range(D_FF // D_MODEL):
        cols = slice(c * D_MODEL, (c + 1) * D_MODEL)
        u = jnp.maximum(jnp.dot(hn, wu_ref[:, cols], preferred_element_type=F32), 0.0)
        y = y + jnp.dot((u * u).astype(BF16), wd_ref[cols, :], preferred_element_type=F32)
    y_ref[...] = y


def _out_mlp(x2d, oc, os_, ow, od, ga, gb, w_o, mlp_g, w_up, w_down, tm):
    m = x2d.shape[0]
    row = pl.BlockSpec((tm, D_MODEL), lambda i: (i, 0))
    const = lambda shape: pl.BlockSpec(shape, lambda i: (0, 0), pipeline_mode=pl.Buffered(1))
    return pl.pallas_call(
        _out_mlp_kernel,
        grid=(m // tm,),
        in_specs=[row] * 7 + [const(w_o.shape), const(mlp_g.shape), const(w_up.shape), const(w_down.shape)],
        out_specs=row,
        out_shape=jax.ShapeDtypeStruct((m, D_MODEL), F32),
        compiler_params=_cparams(("parallel",)),
        name="out_mlp",
    )(x2d, oc, os_, ow, od, ga, gb, w_o, mlp_g, w_up, w_down)


def _win_shift_kernel(n_new, past_ref, new_ref, o_ref):
    keep = o_ref.shape[2]
    o_ref[0] = jnp.concatenate([past_ref[0, :, n_new:keep], new_ref[0, :, 0:n_new]], axis=1)


def _win_shift(past_t, new_t, n_new):
    nb, feats, keep = past_t.shape
    return pl.pallas_call(
        functools.partial(_win_shift_kernel, n_new),
        grid=(nb,),
        in_specs=[pl.BlockSpec((1, feats, keep), lambda b: (b, 0, 0)),
                  pl.BlockSpec((1, feats, new_t.shape[2]), lambda b: (b, 0, 0))],
        out_specs=pl.BlockSpec((1, feats, keep), lambda b: (b, 0, 0)),
        out_shape=jax.ShapeDtypeStruct(past_t.shape, past_t.dtype),
        compiler_params=_cparams(("parallel",)),
        name="win_shift",
    )(past_t, new_t)


TQ_PROMPT = 256
TQ_FLASH = 512
TM_PROMPT = 256
TQ_SAMPLE = 8
TK_PAST = 512


def _round_up(x, m):
    return -(-x // m) * m


def _feature_major(cache):
    lead = cache.shape[:-4]
    p, s, g, dk = cache.shape[-4:]
    n = len(lead)
    return jnp.transpose(cache, tuple(range(n)) + (n + 1, n + 2, n + 3, n)).reshape(lead + (s * g * dk, p))


def _position_major(feat, slots):
    lead = feat.shape[:-2]
    p = feat.shape[-1]
    n = len(lead)
    x = feat.reshape(lead + (slots, NSA_GROUPS, NSA_DK, p))
    return jnp.transpose(x, tuple(range(n)) + (n + 3, n, n + 1, n + 2))


def _layer_params(l, lam_init, rel_bias_table, attn_norm, w_in, nsa_q_gain, nsa_k_gain,
                  cmp_pe_k, cmp_w1_k, cmp_w2_k, cmp_pe_v, cmp_w1_v, cmp_w2_v,
                  diff_q_gain, diff_k_gain, lambda_q1, lambda_k1, lambda_q2, lambda_k2, diff_subln,
                  w_o, mlp_norm, w_up, w_down):
    tile4 = lambda g: jnp.tile(g.astype(F32), MXU_DIM // NSA_DK)
    gains = jnp.zeros((8, MXU_DIM), F32)
    for r, g in enumerate((nsa_q_gain[l], diff_q_gain[l], diff_k_gain[l])):
        gains = gains.at[r].set(tile4(g))
    flat_pe = lambda pe: jnp.broadcast_to(pe.reshape(1, CMP_BLOCK * NSA_DK), (8, CMP_BLOCK * NSA_DK))
    w1 = jnp.stack([cmp_w1_k[l], cmp_w1_v[l]])
    w2 = jnp.stack([cmp_w2_k[l], cmp_w2_v[l]])
    w2z = jnp.zeros((2, NSA_GROUPS, CMP_HID, GD), F32)
    for g in range(NSA_GROUPS):
        w2z = w2z.at[:, g, :, g * NSA_DK:(g + 1) * NSA_DK].set(w2)
    lam = (jnp.exp(jnp.sum(lambda_q1[l].astype(F32) * lambda_k1[l].astype(F32)))
           - jnp.exp(jnp.sum(lambda_q2[l].astype(F32) * lambda_k2[l].astype(F32))) + lam_init)
    w_row, w_kvt = _split_w_in(w_in[l])
    k_gain = tile4(nsa_k_gain[l])
    return dict(
        attn_g=attn_norm[l].reshape(1, D_MODEL), w_row=w_row, w_kvt=w_kvt, bd=_segment_mean_matrix(),
        gains=gains, k_gain=k_gain.reshape(1, GD), k_gain_col=k_gain.reshape(GD, 1),
        pe2=jnp.stack([flat_pe(cmp_pe_k[l]), flat_pe(cmp_pe_v[l])]), w1=w1, w1_bf=w1.astype(BF16),
        w2z=w2z.astype(BF16),
        tab_nsa_t=rel_bias_table[:, :NSA_HEADS].astype(F32).T, tab_diff_t=rel_bias_table[:, NSA_HEADS:].astype(F32).T,
        lam=jnp.broadcast_to(lam.reshape(1, 1), (8, LANE)), subln=diff_subln[l].reshape(1, DIFF_DV),
        w_o=w_o[l].astype(BF16), mlp_g=mlp_norm[l].reshape(1, D_MODEL),
        w_up=w_up[l].astype(BF16), w_down=w_down[l].astype(BF16))


def _run_proj(P, x, tm):
    kgain_t = jnp.broadcast_to(P['k_gain_col'], (GD, tm))
    return _proj(x, P['attn_g'], P['w_row'], P['w_kvt'], P['bd'], P['gains'], kgain_t, tm)


def _mixers(P, lam_init, q, g3, qd, sel_t, win_pieces, kvd, cmp_pages, tail, *,
            t0, tq, tqf, tkn, seq_len, nsa_past, diff_past):
    nb = q.shape[0]
    total = t0 + seq_len
    h0 = _pe_terms(P['pe2'], P['w1'])
    kc, vc = _compress(cmp_pages[0], cmp_pages[1], nb, cmp_pages[2], tail, P['w1_bf'], h0, P['w2z'], P['bd'],
                       P['k_gain'], page_table=cmp_pages[3])
    ncp = kc.shape[1]
    nc = -(-total // CMP_STRIDE) - CMP_BLOCK // CMP_STRIDE + 1
    n_slc = -(-total // SEL_BLOCK)
    nsp = _round_up(n_slc, LANE)
    bias_c, selwt = _cmp_tables(P['tab_nsa_t'], tq, ncp, n_slc, nsp)
    o_cmp, msk = _cmp_topk(q, kc, vc, g3, bias_c, selwt, t0=t0, tq=tq, nc=nc, n_slc=n_slc, n_top=min(N_SEL, n_slc))
    sel_past = dpast = None
    if nsa_past is not None:
        sel_past = (nsa_past[0], nsa_past[1], TK_PAST)
        dpast = (diff_past[0], diff_past[1], TK_PAST)
    o_sel = _sel_attn(q, msk, g3, _band_bias(P['tab_nsa_t']), sel_t, t0=t0, tq=tqf, tkn=tkn, past=sel_past)
    nk = sum(n for _, n, _ in win_pieces)
    o_win = _win_attn(q, g3, _window_bias(P['tab_nsa_t'], tq, nk), win_pieces, t0=t0, tq=tq)
    tkd = kvd.shape[1] if diff_past is not None else tkn
    o_diff = _diff_attn(qd, P['lam'], P['subln'], _band_bias(P['tab_diff_t']), kvd,
                        t0=t0, tq=tqf, tkn=tkd, lam_init=lam_init, past=dpast)
    return o_cmp, o_sel, o_win, o_diff


def _prompt_layer(P, lam_init, x):
    b, t, _ = x.shape
    tq = TQ_PROMPT
    q, nsa_t, win_t, g3, qd, diff5, ga, gb, kvd, sel_t = _run_proj(P, x, TM_PROMPT)
    back = WINDOW // tq
    pieces = [(win_t, tq, functools.partial(lambda k, i: jnp.maximum(i - k, 0), k)) for k in range(back, -1, -1)]
    tail = jnp.zeros((b, 2 * GD, PAGE), F32)
    cmp_pages = (nsa_t, lambda bb, page, pt: (bb, 0, page), t // PAGE, None)
    o_cmp, o_sel, o_win, o_diff = _mixers(
        P, lam_init, q, g3, qd, sel_t, pieces, kvd, cmp_pages, tail,
        t0=0, tq=tq, tqf=min(TQ_FLASH, t), tkn=min(TQ_FLASH, t), seq_len=t, nsa_past=None, diff_past=None)
    f2 = lambda a: a.reshape(b * t, D_MODEL)
    y = _out_mlp(f2(x), f2(o_cmp), f2(o_sel), f2(o_win), f2(o_diff), f2(ga), f2(gb),
                 P['w_o'], P['mlp_g'], P['w_up'], P['w_down'], TM_PROMPT)
    keep = min(WINDOW, t)
    return y.reshape(b, t, D_MODEL), _position_major(nsa_t, 4), diff5, _position_major(win_t[:, :, t - keep:], 2)


def _sample_layer(P, lam_init, x, cache_nsa, cache_diff, cache_win, page_table):
    nb, t, _ = x.shape
    tq = TQ_SAMPLE
    assert t <= tq and cache_win.shape[1] == WINDOW
    past_len = page_table.shape[1] * PAGE
    rows = nb * t
    q, nsa_t, win_t, g3, qd, diff5, ga, gb, kvd, sel_t = _run_proj(P, x.reshape(1, rows, D_MODEL), rows)
    per_seq = lambda a: jnp.pad(a.reshape(nb, t, a.shape[-1]), ((0, 0), (0, tq - t), (0, 0)))
    per_seq_t = lambda a: jnp.pad(jnp.transpose(a.reshape(a.shape[1], nb, t), (1, 0, 2)),
                                  ((0, 0), (0, 0), (0, PAGE - t)))
    nsa_new_t, win_new_t = per_seq_t(nsa_t), per_seq_t(win_t)
    pages_t = _feature_major(cache_nsa)
    past_win_t = _feature_major(cache_win)
    diff_pages = cache_diff.reshape(cache_diff.shape[0], PAGE * 2 * DIFF_HEADS, DIFF_DV)
    pieces = [(past_win_t, WINDOW, lambda i: 0), (win_new_t, PAGE, lambda i: 0)]
    cmp_pages = (pages_t, lambda bb, page, pt: (pt[bb, page], 0, 0), page_table.shape[1], page_table)
    o_cmp, o_sel, o_win, o_diff = _mixers(
        P, lam_init, per_seq(q[0]), per_seq(g3[0]), per_seq(qd[0]), per_seq_t(sel_t), pieces,
        jnp.pad(kvd.reshape(nb, t, 2 * D_MODEL), ((0, 0), (0, PAGE - t), (0, 0))), cmp_pages, nsa_new_t,
        t0=past_len, tq=tq, tqf=tq, tkn=PAGE, seq_len=t,
        nsa_past=(page_table, pages_t), diff_past=(page_table, diff_pages))
    f2 = lambda a: a[:, :t].reshape(rows, D_MODEL)
    y = _out_mlp(x.reshape(rows, D_MODEL), f2(o_cmp), f2(o_sel), f2(o_win), f2(o_diff), ga[0], gb[0],
                 P['w_o'], P['mlp_g'], P['w_up'], P['w_down'], rows)
    win_out_t = _win_shift(past_win_t, win_new_t, t)
    nsa_rows = _position_major(nsa_new_t[:, :, :t], 4)
    return (y.reshape(nb, t, D_MODEL), nsa_rows, diff5.reshape(nb, t, 2, DIFF_HEADS, DIFF_DV),
            _position_major(win_out_t, 2))


def kernel(x_prompt, x_sample, cache_nsa_kv, cache_diff_kv, cache_nsa_win, page_table, rel_bias_table, attn_norm, w_in, nsa_q_gain, nsa_k_gain, cmp_pe_k, cmp_w1_k, cmp_w2_k, cmp_pe_v, cmp_w1_v, cmp_w2_v, diff_q_gain, diff_k_gain, lambda_q1, lambda_k1, lambda_q2, lambda_k2, diff_subln, w_o, mlp_norm, w_up, w_down):
    depth = w_in.shape[0]
    yp, ys = x_prompt, x_sample
    outs = [[] for _ in range(6)]
    for l in range(depth):
        lam_init = 0.8 - 0.6 * math.exp(-0.3 * l)
        P = _layer_params(l, lam_init, rel_bias_table, attn_norm, w_in, nsa_q_gain, nsa_k_gain,
                          cmp_pe_k, cmp_w1_k, cmp_w2_k, cmp_pe_v, cmp_w1_v, cmp_w2_v,
                          diff_q_gain, diff_k_gain, lambda_q1, lambda_k1, lambda_q2, lambda_k2, diff_subln,
                          w_o, mlp_norm, w_up, w_down)
        yp, nsa_p, diff_p, win_p = _prompt_layer(P, lam_init, yp)
        ys, nsa_s, diff_s, win_s = _sample_layer(P, lam_init, ys, cache_nsa_kv[l], cache_diff_kv[l],
                                                 cache_nsa_win[l], page_table)
        for dst, v in zip(outs, (nsa_p, diff_p, win_p, nsa_s, diff_s, win_s)):
            dst.append(v)
    return (yp, ys) + tuple(jnp.stack(o, 0) for o in outs)
```

```python
import functools
import math

import numpy as np
import jax
import jax.numpy as jnp
from jax import lax
from jax.experimental import pallas as pl
from jax.experimental.pallas import tpu as pltpu

F32 = jnp.float32
BF16 = jnp.bfloat16

D_MODEL = 1024
NSA_HEADS = 16
NSA_GROUPS = 4
NSA_HPG = NSA_HEADS // NSA_GROUPS
NSA_DK = D_MODEL // NSA_HEADS
GD = NSA_GROUPS * NSA_DK
CMP_BLOCK = 32
CMP_STRIDE = 16
CMP_HID = 4 * NSA_DK
SEL_BLOCK = 64
N_SEL = 16
WINDOW = 512
FORCE_BONUS = 1e4
DIFF_HEADS = 8
DIFF_DD = D_MODEL // (2 * DIFF_HEADS)
DIFF_DV = 2 * DIFF_DD
D_FF = 4 * D_MODEL
NUM_BUCKETS = 32
MAX_DISTANCE = 128
EPS = 1e-6
NEG_INF = -1e30
TINY = 1e-30
QK_SCALE = NSA_DK ** -0.5

LANE = 128
MXU_DIM = 256
VMEM_LIMIT = 56 * 1024 * 1024
PAGE = 128

C_QN = 0
C_QD = C_QN + D_MODEL
C_KD = C_QD + D_MODEL
C_VD = C_KD + D_MODEL
C_GA = C_VD + D_MODEL
C_GB = C_GA + D_MODEL
C_G3 = C_GB + D_MODEL
N_ROWMAJOR = C_G3 + LANE


def _cparams(sem):
    return pltpu.CompilerParams(dimension_semantics=sem, vmem_limit_bytes=VMEM_LIMIT)


def _t5_bucket_np(dist):
    n = np.maximum(dist, 0)
    max_exact = NUM_BUCKETS // 2
    nf = np.maximum(n, 1).astype(np.float32)
    log_ratio = np.log(nf / np.float32(max_exact)) / np.float32(math.log(MAX_DISTANCE / max_exact))
    large = max_exact + (log_ratio * np.float32(NUM_BUCKETS - max_exact)).astype(np.int32)
    large = np.minimum(large, NUM_BUCKETS - 1)
    return np.where(n < max_exact, n, large).astype(np.int32)


def _segment_mean_matrix():
    i = np.arange(MXU_DIM)
    return jnp.asarray((i[:, None] // NSA_DK == i[None, :] // NSA_DK) / NSA_DK, dtype=BF16)


def _dot_t(a, b, precision=None):
    return lax.dot_general(a, b, (((1,), (1,)), ((), ())), preferred_element_type=F32, precision=precision)


def _bias_lookup_kernel(hp, tab_ref, idx_ref, o_ref):
    n, r, c = idx_ref.shape
    for k in range(n):
        idx = idx_ref[k]
        for hh in range(hp):
            h = pl.program_id(0) * hp + hh
            acc = jnp.zeros((r, c), F32)
            for b in range(NUM_BUCKETS):
                acc = jnp.where(idx == b, tab_ref[h, b], acc)
            o_ref[0, k, hh * r:(hh + 1) * r, :] = acc


def _bias_lookup(tab_t, idx, hp=1):
    heads = tab_t.shape[0]
    n, r, c = idx.shape
    return pl.pallas_call(
        functools.partial(_bias_lookup_kernel, hp),
        grid=(heads // hp,),
        in_specs=[pl.BlockSpec(memory_space=pltpu.SMEM),
                  pl.BlockSpec((n, r, c), lambda g: (0, 0, 0), pipeline_mode=pl.Buffered(1))],
        out_specs=pl.BlockSpec((1, n, hp * r, c), lambda g: (g, 0, 0, 0)),
        out_shape=jax.ShapeDtypeStruct((heads // hp, n, hp * r, c), F32),
        compiler_params=_cparams(("parallel",)),
        name="bias_lookup",
    )(tab_t, jnp.asarray(idx, dtype=jnp.int32))


def _proj_kernel(x_ref, g_ref, w_ref, wt_ref, bd_ref, gains_ref, kgt_ref,
                 q_ref, nsat_ref, wint_ref, g3_ref, qd_ref, diff_ref, ga_ref, gb_ref, kvd_ref, selt_ref):
    x = x_ref[0]
    xn = x * lax.rsqrt(jnp.mean(x * x, axis=-1, keepdims=True) + EPS) * g_ref[...]
    xn = xn.astype(BF16)
    bd = bd_ref[...]

    def mm(c0, width=MXU_DIM):
        return jnp.dot(xn, w_ref[:, c0:c0 + width], preferred_element_type=F32)

    def headnorm(acc, gain_row):
        ms = jnp.dot((acc * acc).astype(BF16), bd, preferred_element_type=F32)
        return acc * lax.rsqrt(ms + EPS) * gains_ref[gain_row:gain_row + 1, :]

    heads_per_tile = MXU_DIM // DIFF_DV
    for c in range(D_MODEL // MXU_DIM):
        o = c * MXU_DIM
        q_ref[0, :, o:o + MXU_DIM] = (headnorm(mm(C_QN + o), 0) * QK_SCALE).astype(BF16)
        qd_ref[0, :, o:o + MXU_DIM] = (headnorm(mm(C_QD + o), 1) * QK_SCALE).astype(BF16)
        kd = headnorm(mm(C_KD + o), 2)
        vd = mm(C_VD + o)
        for hh in range(heads_per_tile):
            h = c * heads_per_tile + hh
            diff_ref[0, :, 0, h, :] = kd[:, hh * DIFF_DV:(hh + 1) * DIFF_DV]
            diff_ref[0, :, 1, h, :] = vd[:, hh * DIFF_DV:(hh + 1) * DIFF_DV]
        kvd_ref[0, :, o:o + MXU_DIM] = kd.astype(BF16)
        kvd_ref[0, :, D_MODEL + o:D_MODEL + o + MXU_DIM] = vd.astype(BF16)
        ga_ref[0, :, o:o + MXU_DIM] = jax.nn.sigmoid(mm(C_GA + o))
        gb_ref[0, :, o:o + MXU_DIM] = jax.nn.sigmoid(mm(C_GB + o))
    g3_ref[0] = jax.nn.sigmoid(mm(C_G3, LANE))

    def mm_t(slot):
        return _dot_t(wt_ref[slot * GD:(slot + 1) * GD, :], xn)

    def headnorm_t(acc):
        ms = jnp.dot(bd, (acc * acc).astype(BF16), preferred_element_type=F32)
        return acc * lax.rsqrt(ms + EPS) * kgt_ref[...]

    nsat_ref[0, 0:GD, :] = mm_t(0)
    nsat_ref[0, GD:2 * GD, :] = mm_t(1)
    k_sel, v_sel = headnorm_t(mm_t(2)), mm_t(3)
    nsat_ref[0, 2 * GD:3 * GD, :] = k_sel
    nsat_ref[0, 3 * GD:4 * GD, :] = v_sel
    selt_ref[0, 0:GD, :] = k_sel.astype(BF16)
    selt_ref[0, GD:2 * GD, :] = v_sel.astype(BF16)
    wint_ref[0, 0:GD, :] = headnorm_t(mm_t(4))
    wint_ref[0, GD:2 * GD, :] = mm_t(5)


def _proj(x, attn_g, w_row, w_kvt, bd, gains, kgain_t, tm):
    nb, t, _ = x.shape
    row = lambda width: pl.BlockSpec((1, tm, width), lambda b, i: (b, i, 0))
    col = lambda feats: pl.BlockSpec((1, feats, tm), lambda b, i: (b, 0, i))
    const = lambda shape: pl.BlockSpec(shape, lambda b, i: (0, 0), pipeline_mode=pl.Buffered(1))
    out_specs = (row(D_MODEL), col(4 * GD), col(2 * GD), row(LANE), row(D_MODEL),
                 pl.BlockSpec((1, tm, 2, DIFF_HEADS, DIFF_DV), lambda b, i: (b, i, 0, 0, 0)),
                 row(D_MODEL), row(D_MODEL), row(2 * D_MODEL), col(2 * GD))
    out_shapes = (
        jax.ShapeDtypeStruct((nb, t, D_MODEL), BF16),
        jax.ShapeDtypeStruct((nb, 4 * GD, t), F32),
        jax.ShapeDtypeStruct((nb, 2 * GD, t), F32),
        jax.ShapeDtypeStruct((nb, t, LANE), F32),
        jax.ShapeDtypeStruct((nb, t, D_MODEL), BF16),
        jax.ShapeDtypeStruct((nb, t, 2, DIFF_HEADS, DIFF_DV), F32),
        jax.ShapeDtypeStruct((nb, t, D_MODEL), F32),
        jax.ShapeDtypeStruct((nb, t, D_MODEL), F32),
        jax.ShapeDtypeStruct((nb, t, 2 * D_MODEL), BF16),
        jax.ShapeDtypeStruct((nb, 2 * GD, t), BF16),
    )
    return pl.pallas_call(
        _proj_kernel,
        grid=(nb, t // tm),
        in_specs=[row(D_MODEL), const((1, D_MODEL)), const(w_row.shape), const(w_kvt.shape),
                  const(bd.shape), const(gains.shape), const(kgain_t.shape)],
        out_specs=out_specs,
        out_shape=out_shapes,
        compiler_params=_cparams(("parallel", "parallel")),
        name="proj",
    )(x, attn_g, w_row, w_kvt, bd, gains, kgain_t)


def _split_w_in(w_in):
    sizes = (D_MODEL, 6 * GD, 3 * NSA_HEADS, D_MODEL, D_MODEL, D_MODEL, D_MODEL, D_MODEL)
    pts = np.cumsum(sizes)[:-1]
    q_n, kv_n, g_n, q_d, k_d, v_d, ga, gb = jnp.split(w_in, pts, axis=-1)
    g_pad = jnp.pad(g_n, ((0, 0), (0, LANE - 3 * NSA_HEADS)))
    w_row = jnp.concatenate([q_n, q_d, k_d, v_d, ga, gb, g_pad], axis=-1).astype(BF16)
    return w_row, kv_n.T.astype(BF16)


CH_PER_PAGE = PAGE // CMP_STRIDE
J_PER_TILE = MXU_DIM // NSA_DK


def _pe_kernel(pe_ref, w1_ref, o_ref):
    for kv in range(2):
        o_ref[kv] = jnp.dot(pe_ref[kv], w1_ref[kv], preferred_element_type=F32,
                            precision=lax.Precision.HIGHEST)


def _pe_terms(pe2, w1):
    return pl.pallas_call(
        _pe_kernel,
        out_shape=jax.ShapeDtypeStruct((2, 8, CMP_HID), F32),
        name="cmp_pe",
    )(pe2, w1)


def _compress_kernel(pps, *refs):
    page_refs = refs[1:1 + pps]
    next_ref, tail_ref, w1_ref, h0_ref, w2_ref, bd_ref, kg_ref, kc_ref, vc_ref = refs[1 + pps:10 + pps]
    rows_refs = refs[10 + pps:]
    is_last = pl.program_id(1) == pl.num_programs(1) - 1
    cb = pps * CH_PER_PAGE
    rows = cb + CH_PER_PAGE
    gpl = LANE // NSA_DK
    for fb, rows_ref in enumerate(rows_refs):
        feats = slice(fb * LANE, (fb + 1) * LANE)
        for k, p in enumerate(page_refs):
            rows_ref[k * PAGE:(k + 1) * PAGE, :] = p[0, feats, :].T
        rows_ref[pps * PAGE:(pps + 1) * PAGE, :] = jnp.where(is_last, tail_ref[0, feats, :], next_ref[0, feats, :]).T

    for kv in range(2):
        acc = [None, None]
        for jt in range(CMP_STRIDE // J_PER_TILE):
            xs = [[rows_refs[kv * (GD // LANE) + half][pl.ds(jt * J_PER_TILE + jj, rows, stride=CMP_STRIDE), :]
                   for half in range(GD // LANE)] for jj in range(J_PER_TILE)]
            lhs = jnp.concatenate(
                [jnp.concatenate([x[g // gpl][:, (g % gpl) * NSA_DK:(g % gpl + 1) * NSA_DK] for x in xs], axis=1)
                 for g in range(NSA_GROUPS)], axis=0).astype(BF16)
            for r in range(2):
                k0 = r * CMP_STRIDE * NSA_DK + jt * MXU_DIM
                part = jnp.dot(lhs, w1_ref[kv, k0:k0 + MXU_DIM, :], preferred_element_type=F32)
                acc[r] = part if acc[r] is None else acc[r] + part
        out = None
        for g in range(NSA_GROUPS):
            h = h0_ref[kv, 0:1, :] + acc[0][g * rows:g * rows + cb] + acc[1][g * rows + 1:g * rows + 1 + cb]
            part = jnp.dot(jax.nn.gelu(h).astype(BF16), w2_ref[kv, g], preferred_element_type=F32)
            out = part if out is None else out + part
        if kv == 0:
            ms = jnp.dot((out * out).astype(BF16), bd_ref[...], preferred_element_type=F32)
            kc_ref[0] = out * lax.rsqrt(ms + EPS) * kg_ref[...]
        else:
            vc_ref[0] = out


def _compress(pages, page_index, nb, npg, tail, w1, h0, w2z, bd, kgain, page_table=None):
    pps = min(16, npg)
    assert npg % pps == 0
    prefetch = [page_table if page_table is not None else jnp.zeros((1, 1), jnp.int32)]
    blk = (1, 2 * GD, PAGE)
    page_spec = lambda k: pl.BlockSpec(blk, lambda b, s, pt: page_index(b, s * pps + k, pt))
    next_spec = pl.BlockSpec(blk, lambda b, s, pt: page_index(b, jnp.minimum(s * pps + pps, npg - 1), pt))
    const = lambda shape: pl.BlockSpec(shape, lambda b, s, pt: (0,) * len(shape), pipeline_mode=pl.Buffered(1))
    out_spec = pl.BlockSpec((1, pps * CH_PER_PAGE, GD), lambda b, s, pt: (b, s, 0))
    out_sds = jax.ShapeDtypeStruct((nb, npg * CH_PER_PAGE, GD), F32)
    return pl.pallas_call(
        functools.partial(_compress_kernel, pps),
        grid_spec=pltpu.PrefetchScalarGridSpec(
            num_scalar_prefetch=1,
            grid=(nb, npg // pps),
            in_specs=[page_spec(k) for k in range(pps)] + [
                next_spec,
                pl.BlockSpec(blk, lambda b, s, pt: (b, 0, 0)),
                const(w1.shape), const(h0.shape), const(w2z.shape), const(bd.shape), const(kgain.shape)],
            out_specs=(out_spec, out_spec),
            scratch_shapes=[pltpu.VMEM(((pps + 1) * PAGE, LANE), F32)] * (2 * GD // LANE)),
        out_shape=(out_sds, out_sds),
        compiler_params=_cparams(("parallel", "arbitrary")),
        name="compress",
    )(*prefetch, *([pages] * (pps + 1)), tail, w1, h0, w2z, bd, kgain)


CMP_WIN = 64
CMP_WIN_BACK = 32
SCORE_PAD = -3.0e38


def _gate_col(g3, head, branch):
    c = head * 3 + branch
    return g3[:, c:c + 1]


def _cmp_topk_kernel(t0, tq, ncp, nc, n_slc, nsp, n_top,
                     q_ref, kc_ref, vc_ref, g3_ref, bias_ref, selwt_ref, o_ref, msk_ref):
    qpos0 = t0 + pl.program_id(1) * tq
    col = lax.broadcasted_iota(jnp.int32, (tq, ncp), 1)
    qpos = qpos0 + lax.broadcasted_iota(jnp.int32, (tq, ncp), 0)
    hidden = jnp.where((col * CMP_STRIDE + (CMP_BLOCK - 1) <= qpos) & (col < nc), 0.0, MASK_NEG)
    any_visible = qpos0 + lax.broadcasted_iota(jnp.int32, (tq, 1), 0) >= CMP_BLOCK - 1
    shift = (qpos0 // CMP_STRIDE - CMP_WIN_BACK + ncp) % ncp
    g3 = g3_ref[0]
    tqp = _round_up(tq, LANE)
    blk = lax.broadcasted_iota(jnp.int32, (nsp, tqp), 0)
    bpos = qpos0 + lax.broadcasted_iota(jnp.int32, (nsp, tqp), 1)
    cur = bpos // SEL_BLOCK
    forced = (blk == 0) | (blk == cur) | (blk == cur - 1)
    valid_b = blk * SEL_BLOCK <= bpos
    outs, carry = [], []
    for g in range(NSA_GROUPS):
        kc_g = kc_ref[0, :, g * NSA_DK:(g + 1) * NSA_DK].astype(BF16)
        vc_g = vc_ref[0, :, g * NSA_DK:(g + 1) * NSA_DK].astype(BF16)
        psum = jnp.zeros((tq, ncp), F32)
        for hl in range(NSA_HPG):
            h = g * NSA_HPG + hl
            q_h = q_ref[0, :, h * NSA_DK:(h + 1) * NSA_DK]
            s = _dot_t(q_h, kc_g) + (pltpu.roll(bias_ref[h, 0], shift, 1) + hidden)
            p = jnp.exp(s - jnp.max(s, axis=1)[:, None])
            inv = jnp.where(any_visible, 1.0 / jnp.maximum(jnp.sum(p, axis=1)[:, None], TINY), 0.0)
            p = p * inv
            o_h = jnp.dot(p.astype(BF16), vc_g, preferred_element_type=F32)
            outs.append(o_h * _gate_col(g3, h, 0))
            psum = psum + p
        if tqp != tq:
            psum = jnp.concatenate([psum, jnp.zeros((tqp - tq, ncp), F32)], axis=0)
        imp_t = _dot_t(selwt_ref[...], psum, precision=lax.Precision.HIGHEST)
        score = jnp.where(valid_b, imp_t + jnp.where(forced, FORCE_BONUS, 0.0), -FORCE_BONUS)
        carry += [jnp.where(blk < n_slc, score, SCORE_PAD), jnp.zeros((nsp, tqp), F32)]

    def pick(_, carry):
        nxt = []
        for g in range(NSA_GROUPS):
            sc, chosen = carry[2 * g], carry[2 * g + 1]
            best = jnp.max(sc, axis=0, keepdims=True)
            first = jnp.min(jnp.where(sc == best, blk, nsp), axis=0, keepdims=True)
            hit = blk == first
            nxt += [jnp.where(hit, SCORE_PAD, sc), jnp.where(hit, 1.0, chosen)]
        return tuple(nxt)

    carry = lax.fori_loop(0, n_top, pick, tuple(carry))
    for g in range(NSA_GROUPS):
        msk_ref[0, :, g * nsp:(g + 1) * nsp] = ((1.0 - carry[2 * g + 1].T[0:tq]) * MASK_NEG).astype(BF16)
    o_ref[0] = jnp.concatenate(outs, axis=1)


def _cmp_topk(q, kc, vc, g3, bias, selwt, *, t0, tq, nc, n_slc, n_top):
    nb, tt, _ = q.shape
    ncp = kc.shape[1]
    nsp = selwt.shape[0]
    assert CMP_STRIDE * ncp > tq + CMP_STRIDE * CMP_WIN_BACK - CMP_BLOCK, (ncp, tq)
    const = lambda shape: pl.BlockSpec(shape, lambda b, i: (0,) * len(shape), pipeline_mode=pl.Buffered(1))
    per_b = lambda shape: pl.BlockSpec(shape, lambda b, i: (b, 0, 0))
    tile = lambda width: pl.BlockSpec((1, tq, width), lambda b, i: (b, i, 0))
    return pl.pallas_call(
        functools.partial(_cmp_topk_kernel, t0, tq, ncp, nc, n_slc, nsp, n_top),
        grid=(nb, tt // tq),
        in_specs=[tile(D_MODEL), per_b((1, ncp, GD)), per_b((1, ncp, GD)), tile(LANE),
                  const(bias.shape), const(selwt.shape)],
        out_specs=(tile(D_MODEL), tile(NSA_GROUPS * nsp)),
        out_shape=(jax.ShapeDtypeStruct((nb, tt, D_MODEL), F32),
                   jax.ShapeDtypeStruct((nb, tt, NSA_GROUPS * nsp), BF16)),
        compiler_params=_cparams(("parallel", "arbitrary")),
        name="cmp_topk",
    )(q, kc, vc, g3, bias, selwt)


def _cmp_tables(tab_nsa_t, tq, ncp, n_slc, nsp):
    r = np.arange(tq)[:, None]
    w = np.arange(ncp)[None, :]
    dc = r - CMP_STRIDE * (w - CMP_WIN_BACK) - (CMP_BLOCK - 1)
    idx = np.where((w < CMP_WIN) & (dc >= 0), _t5_bucket_np(dc), NUM_BUCKETS - 1)
    bias = _bias_lookup(tab_nsa_t, idx[None])
    c_lo = np.arange(ncp)[None, :] * CMP_STRIDE
    s_lo = np.arange(nsp)[:, None] * SEL_BLOCK
    shared = np.clip(np.minimum(c_lo + CMP_BLOCK, s_lo + SEL_BLOCK) - np.maximum(c_lo, s_lo), 0, None)
    shared = np.where(np.arange(nsp)[:, None] < n_slc, shared, 0)
    return bias, jnp.asarray(shared / CMP_BLOCK, dtype=F32)


BIAS_DIAG, BIAS_PREV, BIAS_FAR = 0, 1, 2
T_I, T_J, T_FIRST, T_LAST, T_BIAS, T_NEW, T_KPOS = range(7)


def _pair_tables(nq, tq, t0, tkp, tkn):
    assert t0 % tkp == 0
    n_past = t0 // tkp
    rows = []
    for i in range(nq):
        q0 = t0 + i * tq
        tiles = [(0, j, j * tkp, tkp) for j in range(n_past)]
        tiles += [(1, j, t0 + j * tkn, tkn) for j in range((i * tq + tq - 1) // tkn + 1)]
        for n, (is_new, j, k0, tk) in enumerate(tiles):
            delta = q0 - k0
            assert delta in (0, tk) or delta >= tk + MAX_DISTANCE - 1, (tq, t0, tkp, tkn)
            bsel = BIAS_DIAG if delta == 0 else BIAS_PREV if delta == tk else BIAS_FAR
            rows.append((i, j, int(n == 0), int(n == len(tiles) - 1), bsel, is_new, k0))
    tab = np.array(rows, dtype=np.int32).T.copy()
    cases = sorted({(int(r[T_NEW]), int(r[T_BIAS])) for r in tab.T})
    return jnp.asarray(tab), tuple(cases)


BAND = MAX_DISTANCE


def _band_bias(tab_t):
    r = np.arange(BAND)[:, None]
    c = np.arange(BAND)[None, :]
    tiles = _bias_lookup(tab_t, np.stack([_t5_bucket_np(r - c), _t5_bucket_np(r - c + BAND)]))
    return tiles - tab_t[:, NUM_BUCKETS - 1].reshape(-1, 1, 1, 1)


def _when_cases(tab_ref, p, cases, fn):
    for is_new, bsel in cases:
        pl.when((tab_ref[T_NEW, p] == is_new) & (tab_ref[T_BIAS, p] == bsel))(functools.partial(fn, is_new, bsel))


MASK_NEG = -1e30


def _tile_dtype(rows):
    return BF16 if rows % 16 == 0 else F32


def _band_correction(band_ref, head, tq, tk, delta):
    rb_rows, cb_cols = min(tq, BAND), min(tk, BAND)
    rows = []
    for rb in range(max(tq // BAND, 1)):
        cols = []
        for cb in range(max(tk // BAND, 1)):
            d = delta // BAND + rb - cb
            cols.append(band_ref[head, d, 0:rb_rows, 0:cb_cols] if d in (0, 1) else jnp.zeros((rb_rows, cb_cols), F32))
        rows.append(cols[0] if len(cols) == 1 else jnp.concatenate(cols, axis=1))
    return rows[0] if len(rows) == 1 else jnp.concatenate(rows, axis=0)


def _lane_tiles(x, n):
    return x if n == 1 else jnp.concatenate([x] * n, axis=1)


def _online_softmax_step(s, pv, m_ref, acc_ref):
    m_prev = m_ref[...]
    m_next = jnp.maximum(m_prev, jnp.max(s, axis=1)[:, None])
    p = jnp.exp(s - _lane_tiles(m_next, s.shape[1] // LANE))
    alpha = jnp.exp(m_prev - m_next)
    m_ref[...] = m_next
    acc_ref[...] = _lane_tiles(alpha, acc_ref.shape[1] // LANE) * acc_ref[...] + pv(p.astype(BF16))


def _online_softmax_fused(s_list, pv_list, m_ref, acc_ref):
    n, rows = len(s_list), s_list[0].shape[0]
    width = acc_ref.shape[2]
    s = jnp.concatenate(s_list, axis=0)
    m_prev = m_ref[...].reshape(n * rows, LANE)
    m_next = jnp.maximum(m_prev, jnp.max(s, axis=1)[:, None])
    p = jnp.exp(s - _lane_tiles(m_next, s.shape[1] // LANE)).astype(BF16)
    alpha = jnp.exp(m_prev - m_next)
    m_ref[...] = m_next.reshape(n, rows, LANE)
    pv = jnp.concatenate([pv_list[k](p[k * rows:(k + 1) * rows]) for k in range(n)], axis=0)
    acc = _lane_tiles(alpha, width // LANE) * acc_ref[...].reshape(n * rows, width) + pv
    acc_ref[...] = acc.reshape(n, rows, width)


FUSE_ROWS = 128


def _unit_block(shape, axis):
    return jnp.where(lax.broadcasted_iota(jnp.int32, shape, axis) == 0, 1.0, 0.0).astype(BF16)


def _causal_neg(tq, tk):
    r = lax.broadcasted_iota(jnp.int32, (tq, tk), 0)
    c = lax.broadcasted_iota(jnp.int32, (tq, tk), 1)
    return jnp.where(c <= r, 0.0, MASK_NEG).astype(F32)


def _flash_prologue(tab_ref, p, m_ref, acc_ref):
    @pl.when(tab_ref[T_FIRST, p] == 1)
    def _():
        m_ref[...] = jnp.full(m_ref.shape, NEG_INF, F32)
        acc_ref[...] = jnp.zeros(acc_ref.shape, F32)


def _flash_specs(tq):
    qtile = lambda width: pl.BlockSpec((1, tq, width), lambda b, p, tab, *_: (b, tab[T_I, p], 0))
    const = lambda shape: pl.BlockSpec(shape, lambda b, p, tab, *_: (0,) * len(shape), pipeline_mode=pl.Buffered(1))
    return qtile, const


def _past_page(tab, pt, b, p, ppt, k):
    n_past = pt.shape[1] // ppt
    j = jnp.where(tab[T_NEW, p] == 1, n_past - 1, tab[T_J, p])
    return pt[b, j * ppt + k]


def _sel_kernel(t0, tq, ppt, has_past, nsp, cases, *refs):
    tab_ref = refs[0]
    refs = refs[2 if has_past else 1:]
    q_ref, msk_ref, g3_ref, band_ref, kvn_ref = refs[:5]
    refs = refs[5:]
    if has_past:
        kp_refs, vp_refs = refs[0:ppt], refs[ppt:2 * ppt]
        refs = refs[2 * ppt:]
    o_ref, m_ref, acc_ref, qh_ref = refs
    p = pl.program_id(1)
    kpos0 = tab_ref[T_KPOS, p]
    _flash_prologue(tab_ref, p, m_ref, acc_ref)
    stack = qh_ref.shape[1] // tq
    fuse = NSA_HEADS * tq <= FUSE_ROWS

    @pl.when(tab_ref[T_FIRST, p] == 1)
    def _():
        for hs in range(NSA_HEADS // stack):
            g = hs * stack // NSA_HPG
            q_hs = jnp.concatenate(
                [q_ref[0, :, h * NSA_DK:(h + 1) * NSA_DK] for h in range(hs * stack, (hs + 1) * stack)], axis=0)
            msk_g = msk_ref[0, :, g * nsp:(g + 1) * nsp]
            msk_hs = msk_g if stack == 1 else jnp.concatenate([msk_g] * stack, axis=0)
            qh_ref[hs] = jnp.concatenate([q_hs, msk_hs], axis=1).astype(qh_ref.dtype)

    def step(is_new, bsel):
        if is_new:
            kt_all, vt_all = kvn_ref[0, 0:GD, :], kvn_ref[0, GD:2 * GD, :]
        else:
            kt_all = jnp.concatenate([r[0] for r in kp_refs], axis=1)
            vt_all = jnp.concatenate([r[0] for r in vp_refs], axis=1)
        tk = kt_all.shape[1]
        blk_of_key = (kpos0 + lax.broadcasted_iota(jnp.int32, (nsp, tk), 1)) // SEL_BLOCK
        expand = (lax.broadcasted_iota(jnp.int32, (nsp, tk), 0) == blk_of_key).astype(BF16)
        s_list, pv_list = [], []
        causal = _causal_neg(tq, tk) if bsel == BIAS_DIAG else None
        for g in range(NSA_GROUPS):
            kt_g = jnp.concatenate([kt_all[g * NSA_DK:(g + 1) * NSA_DK, :].astype(BF16), expand], axis=0)
            vt_g = jnp.concatenate([vt_all[g * NSA_DK:(g + 1) * NSA_DK, :].astype(BF16),
                                    _unit_block((LANE - NSA_DK, tk), 0)], axis=0)
            for hs in range(g * NSA_HPG // stack, (g + 1) * NSA_HPG // stack):
                s = jnp.dot(qh_ref[hs].astype(BF16), kt_g, preferred_element_type=F32)
                if bsel != BIAS_FAR:
                    extra = [_band_correction(band_ref, h, tq, tk, 0 if bsel == BIAS_DIAG else tk)
                             for h in range(hs * stack, (hs + 1) * stack)]
                    if causal is not None:
                        extra = [e + causal for e in extra]
                    s = s + (extra[0] if stack == 1 else jnp.concatenate(extra, axis=0))
                pv = functools.partial(lambda vt, pb: _dot_t(pb, vt), vt_g)
                if fuse:
                    s_list.append(s)
                    pv_list.append(pv)
                else:
                    _online_softmax_step(s, pv, m_ref.at[hs], acc_ref.at[hs])
        if fuse:
            _online_softmax_fused(s_list, pv_list, m_ref, acc_ref)

    _when_cases(tab_ref, p, cases, step)

    @pl.when(tab_ref[T_LAST, p] == 1)
    def _():
        g3 = g3_ref[0]
        outs = []
        for h in range(NSA_HEADS):
            rows = slice((h % stack) * tq, (h % stack + 1) * tq)
            o = acc_ref[h // stack, rows, 0:NSA_DK] / jnp.maximum(acc_ref[h // stack, rows, NSA_DK:NSA_DK + 1], TINY)
            outs.append(o * _gate_col(g3, h, 1))
        o_ref[0] = jnp.concatenate(outs, axis=1)


def _sel_attn(q, msk, g3, band, new_t, *, t0, tq, tkn, past=None):
    nb, tt, _ = q.shape
    nsp = msk.shape[2] // NSA_GROUPS
    tkp = past[2] if past else tkn
    ppt = tkp // PAGE
    tab, cases = _pair_tables(tt // tq, tq, t0, tkp, tkn)
    stack = NSA_HPG if NSA_HPG * tq <= LANE else 1
    rows = stack * tq
    qtile, const = _flash_specs(tq)
    new_spec = pl.BlockSpec((1, 2 * GD, tkn), lambda b, p, tab, *_: (b, 0, tab[T_J, p] * tab[T_NEW, p]))
    in_specs = [qtile(D_MODEL), qtile(NSA_GROUPS * nsp), qtile(LANE), const(band.shape), new_spec]
    args = [q, msk, g3, band, new_t]
    prefetch = [tab]
    if past:
        page_spec = lambda slot, k: pl.BlockSpec(
            (1, GD, PAGE), lambda b, p, tab, pt: (_past_page(tab, pt, b, p, ppt, k), slot, 0))
        in_specs += [page_spec(s, k) for s in (2, 3) for k in range(ppt)]
        args += [past[1]] * (2 * ppt)
        prefetch.append(past[0])
    return pl.pallas_call(
        functools.partial(_sel_kernel, t0, tq, ppt, past is not None, nsp, cases),
        grid_spec=pltpu.PrefetchScalarGridSpec(
            num_scalar_prefetch=len(prefetch),
            grid=(nb, tab.shape[1]),
            in_specs=in_specs,
            out_specs=qtile(D_MODEL),
            scratch_shapes=[pltpu.VMEM((NSA_HEADS // stack, rows, LANE), F32),
                            pltpu.VMEM((NSA_HEADS // stack, rows, LANE), F32),
                            pltpu.VMEM((NSA_HEADS // stack, rows, NSA_DK + nsp), _tile_dtype(rows))]),
        out_shape=jax.ShapeDtypeStruct((nb, tt, D_MODEL), F32),
        compiler_params=_cparams(("parallel", "arbitrary")),
        name="sel_attn",
    )(*prefetch, *args)


def _win_kernel(t0, tq, n_piece, *refs):
    q_ref, g3_ref, bias_ref = refs[:3]
    piece_refs = refs[3:3 + n_piece]
    o_ref = refs[3 + n_piece]
    cols = [r[0] for r in piece_refs]
    kvt = cols[0] if n_piece == 1 else jnp.concatenate(cols, axis=1)
    nk = kvt.shape[1]
    qpos0 = t0 + pl.program_id(1) * tq
    r = lax.broadcasted_iota(jnp.int32, (tq, nk), 0)
    c = lax.broadcasted_iota(jnp.int32, (tq, nk), 1)
    dw = r - c + WINDOW
    hidden = jnp.where((qpos0 - WINDOW + c >= 0) & (dw >= 0) & (dw < WINDOW), 0.0, MASK_NEG)
    g3 = g3_ref[0]
    outs = []
    for g in range(NSA_GROUPS):
        kt_g = kvt[g * NSA_DK:(g + 1) * NSA_DK, :].astype(BF16)
        vt_g = jnp.concatenate([kvt[GD + g * NSA_DK:GD + (g + 1) * NSA_DK, :].astype(BF16),
                                _unit_block((LANE - NSA_DK, nk), 0)], axis=0)
        for hl in range(NSA_HPG):
            h = g * NSA_HPG + hl
            q_h = q_ref[0, :, h * NSA_DK:(h + 1) * NSA_DK]
            s = jnp.dot(q_h, kt_g, preferred_element_type=F32) + (bias_ref[h] + hidden)
            m = jnp.maximum(jnp.full((tq, LANE), NEG_INF, F32), jnp.max(s, axis=1)[:, None])
            p = jnp.exp(s - _lane_tiles(m, nk // LANE))
            o = _dot_t(p.astype(BF16), vt_g)
            outs.append(o[:, 0:NSA_DK] / jnp.maximum(o[:, NSA_DK:NSA_DK + 1], TINY) * _gate_col(g3, h, 2))
    o_ref[0] = jnp.concatenate(outs, axis=1)


def _win_attn(q, g3, bias, pieces, *, t0, tq):
    nb, tt, _ = q.shape
    tile = lambda width: pl.BlockSpec((1, tq, width), lambda b, i: (b, i, 0))
    in_specs = [tile(D_MODEL), tile(LANE),
                pl.BlockSpec(bias.shape, lambda b, i: (0, 0, 0), pipeline_mode=pl.Buffered(1))]
    for arr, npos, idx in pieces:
        in_specs.append(pl.BlockSpec((1, 2 * GD, npos), functools.partial(lambda idx, b, i: (b, 0, idx(i)), idx)))
    return pl.pallas_call(
        functools.partial(_win_kernel, t0, tq, len(pieces)),
        grid=(nb, tt // tq),
        in_specs=in_specs,
        out_specs=tile(D_MODEL),
        out_shape=jax.ShapeDtypeStruct((nb, tt, D_MODEL), F32),
        compiler_params=_cparams(("parallel", "arbitrary")),
        name="win_attn",
    )(q, g3, bias, *[a for a, _, _ in pieces])


def _window_bias(tab_nsa_t, tq, nk):
    r = np.arange(tq)[:, None]
    c = np.arange(nk)[None, :]
    return _bias_lookup(tab_nsa_t, _t5_bucket_np(r - c + WINDOW)[None]).reshape(NSA_HEADS, tq, nk)


def _diff_kernel(t0, tq, ppt, has_past, lam_init, cases, *refs):
    tab_ref = refs[0]
    refs = refs[2 if has_past else 1:]
    q_ref, lam_ref, sub_ref, band_ref, kvn_ref = refs[:5]
    refs = refs[5:]
    if has_past:
        kvp_refs = refs[0:ppt]
        refs = refs[ppt:]
    o_ref, m_ref, acc_ref, qq_ref = refs
    p = pl.program_id(1)
    _flash_prologue(tab_ref, p, m_ref, acc_ref)

    @pl.when(tab_ref[T_FIRST, p] == 1)
    def _():
        first_map = lax.broadcasted_iota(jnp.int32, (tq, DIFF_DV), 1) < DIFF_DD
        for h in range(DIFF_HEADS):
            q_h = q_ref[0, :, h * DIFF_DV:(h + 1) * DIFF_DV].astype(F32)
            qq_ref[h] = jnp.concatenate([jnp.where(first_map, q_h, 0.0), jnp.where(first_map, 0.0, q_h)],
                                        axis=0).astype(qq_ref.dtype)

    def page_rows(first_row):
        return jnp.concatenate([r[0, pl.ds(first_row, PAGE, stride=2 * DIFF_HEADS), :] for r in kvp_refs],
                               axis=0).astype(BF16)

    fuse = DIFF_HEADS * 2 * tq <= FUSE_ROWS

    def step(is_new, bsel):
        tk = kvn_ref.shape[1] if is_new else ppt * PAGE
        s_list, pv_list = [], []
        for h in range(DIFF_HEADS):
            if is_new:
                k_h = kvn_ref[0, :, h * DIFF_DV:(h + 1) * DIFF_DV]
                v_h = kvn_ref[0, :, D_MODEL + h * DIFF_DV:D_MODEL + (h + 1) * DIFF_DV]
            else:
                k_h, v_h = page_rows(h), page_rows(DIFF_HEADS + h)
            v_h = jnp.concatenate([v_h, _unit_block((tk, LANE), 1)], axis=1)
            s = _dot_t(qq_ref[h].astype(BF16), k_h)
            if bsel != BIAS_FAR:
                extra = _band_correction(band_ref, h, tq, tk, 0 if bsel == BIAS_DIAG else tk)
                if bsel == BIAS_DIAG:
                    extra = extra + _causal_neg(tq, tk)
                s = s + jnp.concatenate([extra, extra], axis=0)
            pv = functools.partial(lambda v, pb: jnp.dot(pb, v, preferred_element_type=F32), v_h)
            if fuse:
                s_list.append(s)
                pv_list.append(pv)
            else:
                _online_softmax_step(s, pv, m_ref.at[h], acc_ref.at[h])
        if fuse:
            _online_softmax_fused(s_list, pv_list, m_ref, acc_ref)

    _when_cases(tab_ref, p, cases, step)

    @pl.when(tab_ref[T_LAST, p] == 1)
    def _():
        lam = lam_ref[0:1, 0:1]
        for h in range(DIFF_HEADS):
            o = acc_ref[h, :, 0:DIFF_DV] / jnp.maximum(acc_ref[h, :, DIFF_DV:DIFF_DV + 1], TINY)
            o = o[0:tq] - lam * o[tq:2 * tq]
            y = o * lax.rsqrt(jnp.mean(o * o, axis=-1, keepdims=True) + EPS) * sub_ref[...]
            o_ref[0, :, h * DIFF_DV:(h + 1) * DIFF_DV] = y * (1.0 - lam_init)


def _diff_attn(q, lam, subln, band, new_rows, *, t0, tq, tkn, lam_init, past=None):
    nb, tt, _ = q.shape
    tkp = past[2] if past else tkn
    ppt = tkp // PAGE
    tab, cases = _pair_tables(tt // tq, tq, t0, tkp, tkn)
    qtile, const = _flash_specs(tq)
    new_spec = pl.BlockSpec((1, tkn, 2 * D_MODEL), lambda b, p, tab, *_: (b, tab[T_J, p] * tab[T_NEW, p], 0))
    in_specs = [qtile(D_MODEL), const(lam.shape), const(subln.shape), const(band.shape), new_spec]
    args = [q, lam, subln, band, new_rows]
    prefetch = [tab]
    if past:
        page_spec = lambda k: pl.BlockSpec(
            (1, PAGE * 2 * DIFF_HEADS, DIFF_DV), lambda b, p, tab, pt: (_past_page(tab, pt, b, p, ppt, k), 0, 0))
        in_specs += [page_spec(k) for k in range(ppt)]
        args += [past[1]] * ppt
        prefetch.append(past[0])
    return pl.pallas_call(
        functools.partial(_diff_kernel, t0, tq, ppt, past is not None, lam_init, cases),
        grid_spec=pltpu.PrefetchScalarGridSpec(
            num_scalar_prefetch=len(prefetch),
            grid=(nb, tab.shape[1]),
            in_specs=in_specs,
            out_specs=qtile(D_MODEL),
            scratch_shapes=[pltpu.VMEM((DIFF_HEADS, 2 * tq, LANE), F32),
                            pltpu.VMEM((DIFF_HEADS, 2 * tq, DIFF_DV + LANE), F32),
                            pltpu.VMEM((DIFF_HEADS, 2 * tq, DIFF_DV), BF16)]),
        out_shape=jax.ShapeDtypeStruct((nb, tt, D_MODEL), F32),
        compiler_params=_cparams(("parallel", "arbitrary")),
        name="diff_attn",
    )(*prefetch, *args)


def _out_mlp_kernel(x_ref, oc_ref, os_ref, ow_ref, od_ref, ga_ref, gb_ref, wo_ref, g_ref, wu_ref, wd_ref, y_ref):
    o_a = oc_ref[...] + os_ref[...] + ow_ref[...]
    o = ga_ref[...] * o_a + gb_ref[...] * od_ref[...]
    h = x_ref[...] + jnp.dot(o.astype(BF16), wo_ref[...], preferred_element_type=F32)
    hn = (h * lax.rsqrt(jnp.mean(h * h, axis=-1, keepdims=True) + EPS) * g_ref[...]).astype(BF16)
    y = h
    for c in range(D_FF // D_MODEL):
        cols = slice(c * D_MODEL, (c + 1) * D_MODEL)
        u = jnp.maximum(jnp.dot(hn, wu_ref[:, cols], preferred_element_type=F32), 0.0)
        y = y + jnp.dot((u * u).astype(BF16), wd_ref[cols, :], preferred_element_type=F32)
    y_ref[...] = y


def _out_mlp(x2d, oc, os_, ow, od, ga, gb, w_o, mlp_g, w_up, w_down, tm):
    m = x2d.shape[0]
    row = pl.BlockSpec((tm, D_MODEL), lambda i: (i, 0))
    const = lambda shape: pl.BlockSpec(shape, lambda i: (0, 0), pipeline_mode=pl.Buffered(1))
    return pl.pallas_call(
        _out_mlp_kernel,
        grid=(m // tm,),
        in_specs=[row] * 7 + [const(w_o.shape), const(mlp_g.shape), const(w_up.shape), const(w_down.shape)],
        out_specs=row,
        out_shape=jax.ShapeDtypeStruct((m, D_MODEL), F32),
        compiler_params=_cparams(("parallel",)),
        name="out_mlp",
    )(x2d, oc, os_, ow, od, ga, gb, w_o, mlp_g, w_up, w_down)


def _win_shift_kernel(n_new, past_ref, new_ref, o_ref):
    keep = o_ref.shape[2]
    o_ref[0] = jnp.concatenate([past_ref[0, :, n_new:keep], new_ref[0, :, 0:n_new]], axis=1)


def _win_shift(past_t, new_t, n_new):
    nb, feats, keep = past_t.shape
    return pl.pallas_call(
        functools.partial(_win_shift_kernel, n_new),
        grid=(nb,),
        in_specs=[pl.BlockSpec((1, feats, keep), lambda b: (b, 0, 0)),
                  pl.BlockSpec((1, feats, new_t.shape[2]), lambda b: (b, 0, 0))],
        out_specs=pl.BlockSpec((1, feats, keep), lambda b: (b, 0, 0)),
        out_shape=jax.ShapeDtypeStruct(past_t.shape, past_t.dtype),
        compiler_params=_cparams(("parallel",)),
        name="win_shift",
    )(past_t, new_t)


TQ_PROMPT = 256
TQ_FLASH = 512
TM_PROMPT = 256
TQ_SAMPLE = 8
TK_PAST = 512


def _round_up(x, m):
    return -(-x // m) * m


def _feature_major(cache):
    lead = cache.shape[:-4]
    p, s, g, dk = cache.shape[-4:]
    n = len(lead)
    return jnp.transpose(cache, tuple(range(n)) + (n + 1, n + 2, n + 3, n)).reshape(lead + (s * g * dk, p))


def _position_major(feat, slots):
    lead = feat.shape[:-2]
    p = feat.shape[-1]
    n = len(lead)
    x = feat.reshape(lead + (slots, NSA_GROUPS, NSA_DK, p))
    return jnp.transpose(x, tuple(range(n)) + (n + 3, n, n + 1, n + 2))


def _layer_params(l, lam_init, rel_bias_table, attn_norm, w_in, nsa_q_gain, nsa_k_gain,
                  cmp_pe_k, cmp_w1_k, cmp_w2_k, cmp_pe_v, cmp_w1_v, cmp_w2_v,
                  diff_q_gain, diff_k_gain, lambda_q1, lambda_k1, lambda_q2, lambda_k2, diff_subln,
                  w_o, mlp_norm, w_up, w_down):
    tile4 = lambda g: jnp.tile(g.astype(F32), MXU_DIM // NSA_DK)
    gains = jnp.zeros((8, MXU_DIM), F32)
    for r, g in enumerate((nsa_q_gain[l], diff_q_gain[l], diff_k_gain[l])):
        gains = gains.at[r].set(tile4(g))
    flat_pe = lambda pe: jnp.broadcast_to(pe.reshape(1, CMP_BLOCK * NSA_DK), (8, CMP_BLOCK * NSA_DK))
    w1 = jnp.stack([cmp_w1_k[l], cmp_w1_v[l]])
    w2 = jnp.stack([cmp_w2_k[l], cmp_w2_v[l]])
    w2z = jnp.zeros((2, NSA_GROUPS, CMP_HID, GD), F32)
    for g in range(NSA_GROUPS):
        w2z = w2z.at[:, g, :, g * NSA_DK:(g + 1) * NSA_DK].set(w2)
    lam = (jnp.exp(jnp.sum(lambda_q1[l].astype(F32) * lambda_k1[l].astype(F32)))
           - jnp.exp(jnp.sum(lambda_q2[l].astype(F32) * lambda_k2[l].astype(F32))) + lam_init)
    w_row, w_kvt = _split_w_in(w_in[l])
    k_gain = tile4(nsa_k_gain[l])
    return dict(
        attn_g=attn_norm[l].reshape(1, D_MODEL), w_row=w_row, w_kvt=w_kvt, bd=_segment_mean_matrix(),
        gains=gains, k_gain=k_gain.reshape(1, GD), k_gain_col=k_gain.reshape(GD, 1),
        pe2=jnp.stack([flat_pe(cmp_pe_k[l]), flat_pe(cmp_pe_v[l])]), w1=w1, w1_bf=w1.astype(BF16),
        w2z=w2z.astype(BF16),
        tab_nsa_t=rel_bias_table[:, :NSA_HEADS].astype(F32).T, tab_diff_t=rel_bias_table[:, NSA_HEADS:].astype(F32).T,
        lam=jnp.broadcast_to(lam.reshape(1, 1), (8, LANE)), subln=diff_subln[l].reshape(1, DIFF_DV),
        w_o=w_o[l].astype(BF16), mlp_g=mlp_norm[l].reshape(1, D_MODEL),
        w_up=w_up[l].astype(BF16), w_down=w_down[l].astype(BF16))


def _run_proj(P, x, tm):
    kgain_t = jnp.broadcast_to(P['k_gain_col'], (GD, tm))
    return _proj(x, P['attn_g'], P['w_row'], P['w_kvt'], P['bd'], P['gains'], kgain_t, tm)


def _mixers(P, lam_init, q, g3, qd, sel_t, win_pieces, kvd, cmp_pages, tail, *,
            t0, tq, tqf, tkn, seq_len, nsa_past, diff_past):
    nb = q.shape[0]
    total = t0 + seq_len
    h0 = _pe_terms(P['pe2'], P['w1'])
    kc, vc = _compress(cmp_pages[0], cmp_pages[1], nb, cmp_pages[2], tail, P['w1_bf'], h0, P['w2z'], P['bd'],
                       P['k_gain'], page_table=cmp_pages[3])
    ncp = kc.shape[1]
    nc = -(-total // CMP_STRIDE) - CMP_BLOCK // CMP_STRIDE + 1
    n_slc = -(-total // SEL_BLOCK)
    nsp = _round_up(n_slc, LANE)
    bias_c, selwt = _cmp_tables(P['tab_nsa_t'], tqf, ncp, n_slc, nsp)
    o_cmp, msk = _cmp_topk(q, kc, vc, g3, bias_c, selwt, t0=t0, tq=tqf, nc=nc, n_slc=n_slc, n_top=min(N_SEL, n_slc))
    sel_past = dpast = None
    if nsa_past is not None:
        sel_past = (nsa_past[0], nsa_past[1], TK_PAST)
        dpast = (diff_past[0], diff_past[1], TK_PAST)
    o_sel = _sel_attn(q, msk, g3, _band_bias(P['tab_nsa_t']), sel_t, t0=t0, tq=tqf, tkn=tkn, past=sel_past)
    nk = sum(n for _, n, _ in win_pieces)
    o_win = _win_attn(q, g3, _window_bias(P['tab_nsa_t'], tq, nk), win_pieces, t0=t0, tq=tq)
    tkd = kvd.shape[1] if diff_past is not None else tkn
    o_diff = _diff_attn(qd, P['lam'], P['subln'], _band_bias(P['tab_diff_t']), kvd,
                        t0=t0, tq=tqf, tkn=tkd, lam_init=lam_init, past=dpast)
    return o_cmp, o_sel, o_win, o_diff


def _prompt_layer(P, lam_init, x):
    b, t, _ = x.shape
    tq = TQ_PROMPT
    q, nsa_t, win_t, g3, qd, diff5, ga, gb, kvd, sel_t = _run_proj(P, x, TM_PROMPT)
    back = WINDOW // tq
    pieces = [(win_t, tq, functools.partial(lambda k, i: jnp.maximum(i - k, 0), k)) for k in range(back, -1, -1)]
    tail = jnp.zeros((b, 2 * GD, PAGE), F32)
    cmp_pages = (nsa_t, lambda bb, page, pt: (bb, 0, page), t // PAGE, None)
    o_cmp, o_sel, o_win, o_diff = _mixers(
        P, lam_init, q, g3, qd, sel_t, pieces, kvd, cmp_pages, tail,
        t0=0, tq=tq, tqf=min(TQ_FLASH, t), tkn=min(TQ_FLASH, t), seq_len=t, nsa_past=None, diff_past=None)
    f2 = lambda a: a.reshape(b * t, D_MODEL)
    y = _out_mlp(f2(x), f2(o_cmp), f2(o_sel), f2(o_win), f2(o_diff), f2(ga), f2(gb),
                 P['w_o'], P['mlp_g'], P['w_up'], P['w_down'], TM_PROMPT)
    keep = min(WINDOW, t)
    return y.reshape(b, t, D_MODEL), _position_major(nsa_t, 4), diff5, _position_major(win_t[:, :, t - keep:], 2)


def _sample_layer(P, lam_init, x, cache_nsa, cache_diff, cache_win, page_table):
    nb, t, _ = x.shape
    tq = TQ_SAMPLE
    assert t <= tq and cache_win.shape[1] == WINDOW
    past_len = page_table.shape[1] * PAGE
    rows = nb * t
    q, nsa_t, win_t, g3, qd, diff5, ga, gb, kvd, sel_t = _run_proj(P, x.reshape(1, rows, D_MODEL), rows)
    per_seq = lambda a: jnp.pad(a.reshape(nb, t, a.shape[-1]), ((0, 0), (0, tq - t), (0, 0)))
    per_seq_t = lambda a: jnp.pad(jnp.transpose(a.reshape(a.shape[1], nb, t), (1, 0, 2)),
                                  ((0, 0), (0, 0), (0, PAGE - t)))
    nsa_new_t, win_new_t = per_seq_t(nsa_t), per_seq_t(win_t)
    pages_t = _feature_major(cache_nsa)
    past_win_t = _feature_major(cache_win)
    diff_pages = cache_diff.reshape(cache_diff.shape[0], PAGE * 2 * DIFF_HEADS, DIFF_DV)
    pieces = [(past_win_t, WINDOW, lambda i: 0), (win_new_t, PAGE, lambda i: 0)]
    cmp_pages = (pages_t, lambda bb, page, pt: (pt[bb, page], 0, 0), page_table.shape[1], page_table)
    o_cmp, o_sel, o_win, o_diff = _mixers(
        P, lam_init, per_seq(q[0]), per_seq(g3[0]), per_seq(qd[0]), per_seq_t(sel_t), pieces,
        jnp.pad(kvd.reshape(nb, t, 2 * D_MODEL), ((0, 0), (0, PAGE - t), (0, 0))), cmp_pages, nsa_new_t,
        t0=past_len, tq=tq, tqf=tq, tkn=PAGE, seq_len=t,
        nsa_past=(page_table, pages_t), diff_past=(page_table, diff_pages))
    f2 = lambda a: a[:, :t].reshape(rows, D_MODEL)
    y = _out_mlp(x.reshape(rows, D_MODEL), f2(o_cmp), f2(o_sel), f2(o_win), f2(o_diff), ga[0], gb[0],
                 P['w_o'], P['mlp_g'], P['w_up'], P['w_down'], rows)
    win_out_t = _win_shift(past_win_t, win_new_t, t)
    nsa_rows = _position_major(nsa_new_t[:, :, :t], 4)
    return (y.reshape(nb, t, D_MODEL), nsa_rows, diff5.reshape(nb, t, 2, DIFF_HEADS, DIFF_DV),
            _position_major(win_out_t, 2))


def kernel(x_prompt, x_sample, cache_nsa_kv, cache_diff_kv, cache_nsa_win, page_table, rel_bias_table, attn_norm, w_in, nsa_q_gain, nsa_k_gain, cmp_pe_k, cmp_w1_k, cmp_w2_k, cmp_pe_v, cmp_w1_v, cmp_w2_v, diff_q_gain, diff_k_gain, lambda_q1, lambda_k1, lambda_q2, lambda_k2, diff_subln, w_o, mlp_norm, w_up, w_down):
    depth = w_in.shape[0]
    yp, ys = x_prompt, x_sample
    outs = [[] for _ in range(6)]
    for l in range(depth):
        lam_init = 0.8 - 0.6 * math.exp(-0.3 * l)
        P = _layer_params(l, lam_init, rel_bias_table, attn_norm, w_in, nsa_q_gain, nsa_k_gain,
                          cmp_pe_k, cmp_w1_k, cmp_w2_k, cmp_pe_v, cmp_w1_v, cmp_w2_v,
                          diff_q_gain, diff_k_gain, lambda_q1, lambda_k1, lambda_q2, lambda_k2, diff_subln,
                          w_o, mlp_norm, w_up, w_down)
        yp, nsa_p, diff_p, win_p = _prompt_layer(P, lam_init, yp)
        ys, nsa_s, diff_s, win_s = _sample_layer(P, lam_init, ys, cache_nsa_kv[l], cache_diff_kv[l],
                                                 cache_nsa_win[l], page_table)
        for dst, v in zip(outs, (nsa_p, diff_p, win_p, nsa_s, diff_s, win_s)):
            dst.append(v)
    return (yp, ys) + tuple(jnp.stack(o, 0) for o in outs)
```

```python
import functools
import math

import numpy as np
import jax
import jax.numpy as jnp
from jax import lax
from jax.experimental import pallas as pl
from jax.experimental.pallas import tpu as pltpu

F32 = jnp.float32
BF16 = jnp.bfloat16

D_MODEL = 1024
NSA_HEADS = 16
NSA_GROUPS = 4
NSA_HPG = NSA_HEADS // NSA_GROUPS
NSA_DK = D_MODEL // NSA_HEADS
GD = NSA_GROUPS * NSA_DK
CMP_BLOCK = 32
CMP_STRIDE = 16
CMP_HID = 4 * NSA_DK
SEL_BLOCK = 64
N_SEL = 16
WINDOW = 512
FORCE_BONUS = 1e4
DIFF_HEADS = 8
DIFF_DD = D_MODEL // (2 * DIFF_HEADS)
DIFF_DV = 2 * DIFF_DD
D_FF = 4 * D_MODEL
NUM_BUCKETS = 32
MAX_DISTANCE = 128
EPS = 1e-6
NEG_INF = -1e30
TINY = 1e-30
QK_SCALE = NSA_DK ** -0.5

LANE = 128
MXU_DIM = 256
VMEM_LIMIT = 56 * 1024 * 1024
PAGE = 128

C_QN = 0
C_QD = C_QN + D_MODEL
C_KD = C_QD + D_MODEL
C_VD = C_KD + D_MODEL
C_GA = C_VD + D_MODEL
C_GB = C_GA + D_MODEL
C_G3 = C_GB + D_MODEL
N_ROWMAJOR = C_G3 + LANE


def _cparams(sem):
    return pltpu.CompilerParams(dimension_semantics=sem, vmem_limit_bytes=VMEM_LIMIT)


def _t5_bucket_np(dist):
    n = np.maximum(dist, 0)
    max_exact = NUM_BUCKETS // 2
    nf = np.maximum(n, 1).astype(np.float32)
    log_ratio = np.log(nf / np.float32(max_exact)) / np.float32(math.log(MAX_DISTANCE / max_exact))
    large = max_exact + (log_ratio * np.float32(NUM_BUCKETS - max_exact)).astype(np.int32)
    large = np.minimum(large, NUM_BUCKETS - 1)
    return np.where(n < max_exact, n, large).astype(np.int32)


def _segment_mean_matrix():
    i = np.arange(MXU_DIM)
    return jnp.asarray((i[:, None] // NSA_DK == i[None, :] // NSA_DK) / NSA_DK, dtype=BF16)


def _dot_t(a, b, precision=None):
    return lax.dot_general(a, b, (((1,), (1,)), ((), ())), preferred_element_type=F32, precision=precision)


def _bias_lookup_kernel(hp, tab_ref, idx_ref, o_ref):
    n, r, c = idx_ref.shape
    for k in range(n):
        idx = idx_ref[k]
        for hh in range(hp):
            h = pl.program_id(0) * hp + hh
            acc = jnp.zeros((r, c), F32)
            for b in range(NUM_BUCKETS):
                acc = jnp.where(idx == b, tab_ref[h, b], acc)
            o_ref[0, k, hh * r:(hh + 1) * r, :] = acc


def _bias_lookup(tab_t, idx, hp=1):
    heads = tab_t.shape[0]
    n, r, c = idx.shape
    return pl.pallas_call(
        functools.partial(_bias_lookup_kernel, hp),
        grid=(heads // hp,),
        in_specs=[pl.BlockSpec(memory_space=pltpu.SMEM),
                  pl.BlockSpec((n, r, c), lambda g: (0, 0, 0), pipeline_mode=pl.Buffered(1))],
        out_specs=pl.BlockSpec((1, n, hp * r, c), lambda g: (g, 0, 0, 0)),
        out_shape=jax.ShapeDtypeStruct((heads // hp, n, hp * r, c), F32),
        compiler_params=_cparams(("parallel",)),
        name="bias_lookup",
    )(tab_t, jnp.asarray(idx, dtype=jnp.int32))


def _proj_kernel(x_ref, g_ref, w_ref, wt_ref, bd_ref, gains_ref, kgt_ref,
                 q_ref, nsat_ref, wint_ref, g3_ref, qd_ref, diff_ref, ga_ref, gb_ref, kvd_ref, selt_ref):
    x = x_ref[0]
    xn = x * lax.rsqrt(jnp.mean(x * x, axis=-1, keepdims=True) + EPS) * g_ref[...]
    xn = xn.astype(BF16)
    bd = bd_ref[...]

    def mm(c0, width=MXU_DIM):
        return jnp.dot(xn, w_ref[:, c0:c0 + width], preferred_element_type=F32)

    def headnorm(acc, gain_row):
        ms = jnp.dot((acc * acc).astype(BF16), bd, preferred_element_type=F32)
        return acc * lax.rsqrt(ms + EPS) * gains_ref[gain_row:gain_row + 1, :]

    heads_per_tile = MXU_DIM // DIFF_DV
    for c in range(D_MODEL // MXU_DIM):
        o = c * MXU_DIM
        q_ref[0, :, o:o + MXU_DIM] = (headnorm(mm(C_QN + o), 0) * QK_SCALE).astype(BF16)
        qd_ref[0, :, o:o + MXU_DIM] = (headnorm(mm(C_QD + o), 1) * QK_SCALE).astype(BF16)
        kd = headnorm(mm(C_KD + o), 2)
        vd = mm(C_VD + o)
        for hh in range(heads_per_tile):
            h = c * heads_per_tile + hh
            diff_ref[0, :, 0, h, :] = kd[:, hh * DIFF_DV:(hh + 1) * DIFF_DV]
            diff_ref[0, :, 1, h, :] = vd[:, hh * DIFF_DV:(hh + 1) * DIFF_DV]
        kvd_ref[0, :, o:o + MXU_DIM] = kd.astype(BF16)
        kvd_ref[0, :, D_MODEL + o:D_MODEL + o + MXU_DIM] = vd.astype(BF16)
        ga_ref[0, :, o:o + MXU_DIM] = jax.nn.sigmoid(mm(C_GA + o))
        gb_ref[0, :, o:o + MXU_DIM] = jax.nn.sigmoid(mm(C_GB + o))
    g3_ref[0] = jax.nn.sigmoid(mm(C_G3, LANE))

    def mm_t(slot):
        return _dot_t(wt_ref[slot * GD:(slot + 1) * GD, :], xn)

    def headnorm_t(acc):
        ms = jnp.dot(bd, (acc * acc).astype(BF16), preferred_element_type=F32)
        return acc * lax.rsqrt(ms + EPS) * kgt_ref[...]

    nsat_ref[0, 0:GD, :] = mm_t(0)
    nsat_ref[0, GD:2 * GD, :] = mm_t(1)
    k_sel, v_sel = headnorm_t(mm_t(2)), mm_t(3)
    nsat_ref[0, 2 * GD:3 * GD, :] = k_sel
    nsat_ref[0, 3 * GD:4 * GD, :] = v_sel
    selt_ref[0, 0:GD, :] = k_sel.astype(BF16)
    selt_ref[0, GD:2 * GD, :] = v_sel.astype(BF16)
    wint_ref[0, 0:GD, :] = headnorm_t(mm_t(4))
    wint_ref[0, GD:2 * GD, :] = mm_t(5)


def _proj(x, attn_g, w_row, w_kvt, bd, gains, kgain_t, tm):
    nb, t, _ = x.shape
    row = lambda width: pl.BlockSpec((1, tm, width), lambda b, i: (b, i, 0))
    col = lambda feats: pl.BlockSpec((1, feats, tm), lambda b, i: (b, 0, i))
    const = lambda shape: pl.BlockSpec(shape, lambda b, i: (0, 0), pipeline_mode=pl.Buffered(1))
    out_specs = (row(D_MODEL), col(4 * GD), col(2 * GD), row(LANE), row(D_MODEL),
                 pl.BlockSpec((1, tm, 2, DIFF_HEADS, DIFF_DV), lambda b, i: (b, i, 0, 0, 0)),
                 row(D_MODEL), row(D_MODEL), row(2 * D_MODEL), col(2 * GD))
    out_shapes = (
        jax.ShapeDtypeStruct((nb, t, D_MODEL), BF16),
        jax.ShapeDtypeStruct((nb, 4 * GD, t), F32),
        jax.ShapeDtypeStruct((nb, 2 * GD, t), F32),
        jax.ShapeDtypeStruct((nb, t, LANE), F32),
        jax.ShapeDtypeStruct((nb, t, D_MODEL), BF16),
        jax.ShapeDtypeStruct((nb, t, 2, DIFF_HEADS, DIFF_DV), F32),
        jax.ShapeDtypeStruct((nb, t, D_MODEL), F32),
        jax.ShapeDtypeStruct((nb, t, D_MODEL), F32),
        jax.ShapeDtypeStruct((nb, t, 2 * D_MODEL), BF16),
        jax.ShapeDtypeStruct((nb, 2 * GD, t), BF16),
    )
    return pl.pallas_call(
        _proj_kernel,
        grid=(nb, t // tm),
        in_specs=[row(D_MODEL), const((1, D_MODEL)), const(w_row.shape), const(w_kvt.shape),
                  const(bd.shape), const(gains.shape), const(kgain_t.shape)],
        out_specs=out_specs,
        out_shape=out_shapes,
        compiler_params=_cparams(("parallel", "parallel")),
        name="proj",
    )(x, attn_g, w_row, w_kvt, bd, gains, kgain_t)


def _split_w_in(w_in):
    sizes = (D_MODEL, 6 * GD, 3 * NSA_HEADS, D_MODEL, D_MODEL, D_MODEL, D_MODEL, D_MODEL)
    pts = np.cumsum(sizes)[:-1]
    q_n, kv_n, g_n, q_d, k_d, v_d, ga, gb = jnp.split(w_in, pts, axis=-1)
    g_pad = jnp.pad(g_n, ((0, 0), (0, LANE - 3 * NSA_HEADS)))
    w_row = jnp.concatenate([q_n, q_d, k_d, v_d, ga, gb, g_pad], axis=-1).astype(BF16)
    return w_row, kv_n.T.astype(BF16)


CH_PER_PAGE = PAGE // CMP_STRIDE
J_PER_TILE = MXU_DIM // NSA_DK


def _pe_kernel(pe_ref, w1_ref, o_ref):
    for kv in range(2):
        o_ref[kv] = jnp.dot(pe_ref[kv], w1_ref[kv], preferred_element_type=F32,
                            precision=lax.Precision.HIGHEST)


def _pe_terms(pe2, w1):
    return pl.pallas_call(
        _pe_kernel,
        out_shape=jax.ShapeDtypeStruct((2, 8, CMP_HID), F32),
        name="cmp_pe",
    )(pe2, w1)


def _compress_kernel(pps, *refs):
    page_refs = refs[1:1 + pps]
    next_ref, tail_ref, w1_ref, h0_ref, w2_ref, bd_ref, kg_ref, kc_ref, vc_ref = refs[1 + pps:10 + pps]
    rows_refs = refs[10 + pps:]
    is_last = pl.program_id(1) == pl.num_programs(1) - 1
    cb = pps * CH_PER_PAGE
    rows = cb + CH_PER_PAGE
    gpl = LANE // NSA_DK
    for fb, rows_ref in enumerate(rows_refs):
        feats = slice(fb * LANE, (fb + 1) * LANE)
        for k, p in enumerate(page_refs):
            rows_ref[k * PAGE:(k + 1) * PAGE, :] = p[0, feats, :].T
        rows_ref[pps * PAGE:(pps + 1) * PAGE, :] = jnp.where(is_last, tail_ref[0, feats, :], next_ref[0, feats, :]).T

    for kv in range(2):
        acc = [None, None]
        for jt in range(CMP_STRIDE // J_PER_TILE):
            xs = [[rows_refs[kv * (GD // LANE) + half][pl.ds(jt * J_PER_TILE + jj, rows, stride=CMP_STRIDE), :]
                   for half in range(GD // LANE)] for jj in range(J_PER_TILE)]
            lhs = jnp.concatenate(
                [jnp.concatenate([x[g // gpl][:, (g % gpl) * NSA_DK:(g % gpl + 1) * NSA_DK] for x in xs], axis=1)
                 for g in range(NSA_GROUPS)], axis=0).astype(BF16)
            for r in range(2):
                k0 = r * CMP_STRIDE * NSA_DK + jt * MXU_DIM
                part = jnp.dot(lhs, w1_ref[kv, k0:k0 + MXU_DIM, :], preferred_element_type=F32)
                acc[r] = part if acc[r] is None else acc[r] + part
        out = None
        for g in range(NSA_GROUPS):
            h = h0_ref[kv, 0:1, :] + acc[0][g * rows:g * rows + cb] + acc[1][g * rows + 1:g * rows + 1 + cb]
            part = jnp.dot(jax.nn.gelu(h).astype(BF16), w2_ref[kv, g], preferred_element_type=F32)
            out = part if out is None else out + part
        if kv == 0:
            ms = jnp.dot((out * out).astype(BF16), bd_ref[...], preferred_element_type=F32)
            kc_ref[0] = (out * lax.rsqrt(ms + EPS) * kg_ref[...]).T
        else:
            vc_ref[0] = out


def _compress(pages, page_index, nb, npg, tail, w1, h0, w2z, bd, kgain, page_table=None):
    pps = min(16, npg)
    assert npg % pps == 0
    prefetch = [page_table if page_table is not None else jnp.zeros((1, 1), jnp.int32)]
    blk = (1, 2 * GD, PAGE)
    page_spec = lambda k: pl.BlockSpec(blk, lambda b, s, pt: page_index(b, s * pps + k, pt))
    next_spec = pl.BlockSpec(blk, lambda b, s, pt: page_index(b, jnp.minimum(s * pps + pps, npg - 1), pt))
    const = lambda shape: pl.BlockSpec(shape, lambda b, s, pt: (0,) * len(shape), pipeline_mode=pl.Buffered(1))
    out_spec = pl.BlockSpec((1, pps * CH_PER_PAGE, GD), lambda b, s, pt: (b, s, 0))
    out_sds = jax.ShapeDtypeStruct((nb, npg * CH_PER_PAGE, GD), F32)
    return pl.pallas_call(
        functools.partial(_compress_kernel, pps),
        grid_spec=pltpu.PrefetchScalarGridSpec(
            num_scalar_prefetch=1,
            grid=(nb, npg // pps),
            in_specs=[page_spec(k) for k in range(pps)] + [
                next_spec,
                pl.BlockSpec(blk, lambda b, s, pt: (b, 0, 0)),
                const(w1.shape), const(h0.shape), const(w2z.shape), const(bd.shape), const(kgain.shape)],
            out_specs=(pl.BlockSpec((1, GD, pps * CH_PER_PAGE), lambda b, s, pt: (b, 0, s)), out_spec),
            scratch_shapes=[pltpu.VMEM(((pps + 1) * PAGE, LANE), F32)] * (2 * GD // LANE)),
        out_shape=(jax.ShapeDtypeStruct((nb, GD, npg * CH_PER_PAGE), F32), out_sds),
        compiler_params=_cparams(("parallel", "arbitrary")),
        name="compress",
    )(*prefetch, *([pages] * (pps + 1)), tail, w1, h0, w2z, bd, kgain)


CMP_WIN = 64
CMP_WIN_BACK = 32
SCORE_PAD = -(2.0 ** 126)


def _gate_col(g3, head, branch):
    c = head * 3 + branch
    return g3[:, c:c + 1]


def _cmp_topk_kernel(t0, tq, ncp, nc, n_slc, nsp, n_top,
                     q_ref, kct_ref, vc_ref, g3_ref, bias_ref, selwt_ref, o_ref, msk_ref):
    qpos0 = t0 + pl.program_id(1) * tq
    col = lax.broadcasted_iota(jnp.int32, (tq, ncp), 1)
    qpos = qpos0 + lax.broadcasted_iota(jnp.int32, (tq, ncp), 0)
    hidden = jnp.where((col * CMP_STRIDE + (CMP_BLOCK - 1) <= qpos) & (col < nc), 0.0, MASK_NEG)
    any_visible = qpos0 + lax.broadcasted_iota(jnp.int32, (tq, 1), 0) >= CMP_BLOCK - 1
    c0 = qpos0 // CMP_STRIDE - CMP_WIN_BACK
    place = (lax.broadcasted_iota(jnp.int32, (CMP_WIN, ncp), 1)
             == lax.broadcasted_iota(jnp.int32, (CMP_WIN, ncp), 0) + c0).astype(BF16)
    g3 = g3_ref[0]
    n_lanes = _round_up(NSA_GROUPS * tq, LANE)
    blk = lax.broadcasted_iota(jnp.int32, (nsp, n_lanes), 0)
    bpos = qpos0 + lax.broadcasted_iota(jnp.int32, (nsp, n_lanes), 1) % tq
    cur = bpos // SEL_BLOCK
    forced = (blk == 0) | (blk == cur) | (blk == cur - 1)
    valid_b = blk * SEL_BLOCK <= bpos
    outs, psums = [], []
    for g in range(NSA_GROUPS):
        kc_g = jnp.concatenate([place, place, kct_ref[0, g * NSA_DK:(g + 1) * NSA_DK, :].astype(BF16)], axis=0)
        vc_g = vc_ref[0, :, g * NSA_DK:(g + 1) * NSA_DK].astype(BF16)
        psum = jnp.zeros((tq, ncp), F32)
        for hl in range(NSA_HPG):
            h = g * NSA_HPG + hl
            q_h = jnp.concatenate([bias_ref[h], q_ref[0, :, h * NSA_DK:(h + 1) * NSA_DK]], axis=1)
            s = jnp.dot(q_h, kc_g, preferred_element_type=F32) + hidden
            p = jnp.exp(s - jnp.max(s, axis=1)[:, None])
            inv = jnp.where(any_visible, 1.0 / jnp.maximum(jnp.sum(p, axis=1)[:, None], TINY), 0.0)
            p = p * inv
            o_h = jnp.dot(p.astype(BF16), vc_g, preferred_element_type=F32)
            outs.append(o_h * _gate_col(g3, h, 0))
            psum = psum + p
        psums.append(psum)
    if n_lanes != NSA_GROUPS * tq:
        psums.append(jnp.zeros((n_lanes - NSA_GROUPS * tq, ncp), F32))
    imp_t = _dot_t(selwt_ref[...], jnp.concatenate(psums, axis=0), precision=lax.Precision.HIGHEST)
    score = jnp.where(valid_b, imp_t + jnp.where(forced, FORCE_BONUS, 0.0), -FORCE_BONUS)
    score = jnp.where(blk < n_slc, score, SCORE_PAD)

    def pick(_, sc):
        best = jnp.max(sc, axis=0, keepdims=True)
        first = jnp.min(jnp.where(sc == best, blk, nsp), axis=0, keepdims=True)
        return jnp.where(blk == first, SCORE_PAD, sc)

    picked = lax.fori_loop(0, n_top, pick, score)
    hidden_blocks = jnp.where((picked < 0.5 * SCORE_PAD) & (blk < n_slc), 0.0, MASK_NEG).T
    for g in range(NSA_GROUPS):
        msk_ref[0, :, g * nsp:(g + 1) * nsp] = hidden_blocks[g * tq:(g + 1) * tq].astype(BF16)
    o_ref[0] = jnp.concatenate(outs, axis=1)


def _cmp_topk(q, kct, vc, g3, bias, selwt, *, t0, tq, nc, n_slc, n_top):
    nb, tt, _ = q.shape
    ncp = vc.shape[1]
    nsp = selwt.shape[0]
    const = lambda shape: pl.BlockSpec(shape, lambda b, i: (0,) * len(shape), pipeline_mode=pl.Buffered(1))
    per_b = lambda shape: pl.BlockSpec(shape, lambda b, i: (b, 0, 0))
    tile = lambda width: pl.BlockSpec((1, tq, width), lambda b, i: (b, i, 0))
    return pl.pallas_call(
        functools.partial(_cmp_topk_kernel, t0, tq, ncp, nc, n_slc, nsp, n_top),
        grid=(nb, tt // tq),
        in_specs=[tile(D_MODEL), per_b((1, GD, ncp)), per_b((1, ncp, GD)), tile(LANE),
                  const(bias.shape), const(selwt.shape)],
        out_specs=(tile(D_MODEL), tile(NSA_GROUPS * nsp)),
        out_shape=(jax.ShapeDtypeStruct((nb, tt, D_MODEL), F32),
                   jax.ShapeDtypeStruct((nb, tt, NSA_GROUPS * nsp), BF16)),
        compiler_params=_cparams(("parallel", "arbitrary")),
        name="cmp_topk",
    )(q, kct, vc, g3, bias, selwt)


def _cmp_tables(tab_nsa_t, tq, ncp, n_slc, nsp):
    r = np.arange(tq)[:, None]
    w = np.arange(CMP_WIN)[None, :]
    dc = r - CMP_STRIDE * (w - CMP_WIN_BACK) - (CMP_BLOCK - 1)
    idx = np.where(dc >= 0, _t5_bucket_np(dc), NUM_BUCKETS - 1)
    win = _bias_lookup(tab_nsa_t, idx[None])[:, 0] - tab_nsa_t[:, NUM_BUCKETS - 1].reshape(-1, 1, 1)
    high = win.astype(BF16)
    bias = jnp.concatenate([high, (win - high.astype(F32)).astype(BF16)], axis=-1)
    c_lo = np.arange(ncp)[None, :] * CMP_STRIDE
    s_lo = np.arange(nsp)[:, None] * SEL_BLOCK
    shared = np.clip(np.minimum(c_lo + CMP_BLOCK, s_lo + SEL_BLOCK) - np.maximum(c_lo, s_lo), 0, None)
    shared = np.where(np.arange(nsp)[:, None] < n_slc, shared, 0)
    return bias, jnp.asarray(shared / CMP_BLOCK, dtype=F32)


BIAS_DIAG, BIAS_PREV, BIAS_FAR = 0, 1, 2
T_I, T_J, T_FIRST, T_LAST, T_BIAS, T_NEW, T_KPOS = range(7)


def _pair_tables(nq, tq, t0, tkp, tkn):
    assert t0 % tkp == 0
    n_past = t0 // tkp
    rows = []
    for i in range(nq):
        q0 = t0 + i * tq
        tiles = [(0, j, j * tkp, tkp) for j in range(n_past)]
        tiles += [(1, j, t0 + j * tkn, tkn) for j in range((i * tq + tq - 1) // tkn + 1)]
        for n, (is_new, j, k0, tk) in enumerate(tiles):
            delta = q0 - k0
            assert delta in (0, tk) or delta >= tk + MAX_DISTANCE - 1, (tq, t0, tkp, tkn)
            bsel = BIAS_DIAG if delta == 0 else BIAS_PREV if delta == tk else BIAS_FAR
            rows.append((i, j, int(n == 0), int(n == len(tiles) - 1), bsel, is_new, k0))
    tab = np.array(rows, dtype=np.int32).T.copy()
    cases = sorted({(int(r[T_NEW]), int(r[T_BIAS])) for r in tab.T})
    return jnp.asarray(tab), tuple(cases)


BAND = MAX_DISTANCE


def _band_bias(tab_t):
    r = np.arange(BAND)[:, None]
    c = np.arange(BAND)[None, :]
    tiles = _bias_lookup(tab_t, np.stack([_t5_bucket_np(r - c), _t5_bucket_np(r - c + BAND)]))
    return tiles - tab_t[:, NUM_BUCKETS - 1].reshape(-1, 1, 1, 1)


def _when_cases(tab_ref, p, cases, fn):
    for is_new, bsel in cases:
        pl.when((tab_ref[T_NEW, p] == is_new) & (tab_ref[T_BIAS, p] == bsel))(functools.partial(fn, is_new, bsel))


MASK_NEG = -1e30


def _tile_dtype(rows):
    return BF16 if rows % 16 == 0 else F32


def _band_correction(band_ref, head, tq, tk, delta):
    rb_rows, cb_cols = min(tq, BAND), min(tk, BAND)
    rows = []
    for rb in range(max(tq // BAND, 1)):
        cols = []
        for cb in range(max(tk // BAND, 1)):
            d = delta // BAND + rb - cb
            cols.append(band_ref[head, d, 0:rb_rows, 0:cb_cols] if d in (0, 1) else jnp.zeros((rb_rows, cb_cols), F32))
        rows.append(cols[0] if len(cols) == 1 else jnp.concatenate(cols, axis=1))
    return rows[0] if len(rows) == 1 else jnp.concatenate(rows, axis=0)


def _lane_tiles(x, n):
    return x if n == 1 else jnp.concatenate([x] * n, axis=1)


def _online_softmax_step(s, pv, m_ref, acc_ref):
    m_prev = m_ref[...]
    m_next = jnp.maximum(m_prev, jnp.max(s, axis=1)[:, None])
    p = jnp.exp(s - _lane_tiles(m_next, s.shape[1] // LANE))
    alpha = jnp.exp(m_prev - m_next)
    m_ref[...] = m_next
    acc_ref[...] = _lane_tiles(alpha, acc_ref.shape[1] // LANE) * acc_ref[...] + pv(p.astype(BF16))


def _online_softmax_fused(s_list, pv_list, m_ref, acc_ref):
    n, rows = len(s_list), s_list[0].shape[0]
    width = acc_ref.shape[2]
    s = jnp.concatenate(s_list, axis=0)
    m_prev = m_ref[...].reshape(n * rows, LANE)
    m_next = jnp.maximum(m_prev, jnp.max(s, axis=1)[:, None])
    p = jnp.exp(s - _lane_tiles(m_next, s.shape[1] // LANE)).astype(BF16)
    alpha = jnp.exp(m_prev - m_next)
    m_ref[...] = m_next.reshape(n, rows, LANE)
    pv = jnp.concatenate([pv_list[k](p[k * rows:(k + 1) * rows]) for k in range(n)], axis=0)
    acc = _lane_tiles(alpha, width // LANE) * acc_ref[...].reshape(n * rows, width) + pv
    acc_ref[...] = acc.reshape(n, rows, width)


FUSE_ROWS = 128


def _unit_block(shape, axis):
    return jnp.where(lax.broadcasted_iota(jnp.int32, shape, axis) == 0, 1.0, 0.0).astype(BF16)


def _causal_neg(tq, tk):
    r = lax.broadcasted_iota(jnp.int32, (tq, tk), 0)
    c = lax.broadcasted_iota(jnp.int32, (tq, tk), 1)
    return jnp.where(c <= r, 0.0, MASK_NEG).astype(F32)


def _flash_prologue(tab_ref, p, m_ref, acc_ref):
    @pl.when(tab_ref[T_FIRST, p] == 1)
    def _():
        m_ref[...] = jnp.full(m_ref.shape, NEG_INF, F32)
        acc_ref[...] = jnp.zeros(acc_ref.shape, F32)


def _flash_specs(tq):
    qtile = lambda width: pl.BlockSpec((1, tq, width), lambda b, p, tab, *_: (b, tab[T_I, p], 0))
    const = lambda shape: pl.BlockSpec(shape, lambda b, p, tab, *_: (0,) * len(shape), pipeline_mode=pl.Buffered(1))
    return qtile, const


def _past_page(tab, pt, b, p, ppt, k):
    n_past = pt.shape[1] // ppt
    j = jnp.where(tab[T_NEW, p] == 1, n_past - 1, tab[T_J, p])
    return pt[b, j * ppt + k]


def _sel_kernel(t0, tq, ppt, has_past, nsp, cases, *refs):
    tab_ref = refs[0]
    refs = refs[2 if has_past else 1:]
    q_ref, msk_ref, g3_ref, band_ref, kvn_ref = refs[:5]
    refs = refs[5:]
    if has_past:
        kp_refs, vp_refs = refs[0:ppt], refs[ppt:2 * ppt]
        refs = refs[2 * ppt:]
    o_ref, m_ref, acc_ref, qh_ref = refs
    p = pl.program_id(1)
    kpos0 = tab_ref[T_KPOS, p]
    _flash_prologue(tab_ref, p, m_ref, acc_ref)
    stack = qh_ref.shape[1] // tq
    fuse = NSA_HEADS * tq <= FUSE_ROWS

    @pl.when(tab_ref[T_FIRST, p] == 1)
    def _():
        for hs in range(NSA_HEADS // stack):
            g = hs * stack // NSA_HPG
            q_hs = jnp.concatenate(
                [q_ref[0, :, h * NSA_DK:(h + 1) * NSA_DK] for h in range(hs * stack, (hs + 1) * stack)], axis=0)
            msk_g = msk_ref[0, :, g * nsp:(g + 1) * nsp]
            msk_hs = msk_g if stack == 1 else jnp.concatenate([msk_g] * stack, axis=0)
            qh_ref[hs] = jnp.concatenate([q_hs, msk_hs], axis=1).astype(qh_ref.dtype)

    def step(is_new, bsel):
        if is_new:
            kt_all, vt_all = kvn_ref[0, 0:GD, :], kvn_ref[0, GD:2 * GD, :]
        else:
            kt_all = jnp.concatenate([r[0] for r in kp_refs], axis=1)
            vt_all = jnp.concatenate([r[0] for r in vp_refs], axis=1)
        tk = kt_all.shape[1]
        blk_of_key = (kpos0 + lax.broadcasted_iota(jnp.int32, (nsp, tk), 1)) // SEL_BLOCK
        expand = (lax.broadcasted_iota(jnp.int32, (nsp, tk), 0) == blk_of_key).astype(BF16)
        s_list, pv_list = [], []
        causal = _causal_neg(tq, tk) if bsel == BIAS_DIAG else None
        for g in range(NSA_GROUPS):
            kt_g = jnp.concatenate([kt_all[g * NSA_DK:(g + 1) * NSA_DK, :].astype(BF16), expand], axis=0)
            vt_g = jnp.concatenate([vt_all[g * NSA_DK:(g + 1) * NSA_DK, :].astype(BF16),
                                    _unit_block((LANE - NSA_DK, tk), 0)], axis=0)
            for hs in range(g * NSA_HPG // stack, (g + 1) * NSA_HPG // stack):
                s = jnp.dot(qh_ref[hs].astype(BF16), kt_g, preferred_element_type=F32)
                if bsel != BIAS_FAR:
                    extra = [_band_correction(band_ref, h, tq, tk, 0 if bsel == BIAS_DIAG else tk)
                             for h in range(hs * stack, (hs + 1) * stack)]
                    if causal is not None:
                        extra = [e + causal for e in extra]
                    s = s + (extra[0] if stack == 1 else jnp.concatenate(extra, axis=0))
                pv = functools.partial(lambda vt, pb: _dot_t(pb, vt), vt_g)
                if fuse:
                    s_list.append(s)
                    pv_list.append(pv)
                else:
                    _online_softmax_step(s, pv, m_ref.at[hs], acc_ref.at[hs])
        if fuse:
            _online_softmax_fused(s_list, pv_list, m_ref, acc_ref)

    _when_cases(tab_ref, p, cases, step)

    @pl.when(tab_ref[T_LAST, p] == 1)
    def _():
        g3 = g3_ref[0]
        outs = []
        for h in range(NSA_HEADS):
            rows = slice((h % stack) * tq, (h % stack + 1) * tq)
            o = acc_ref[h // stack, rows, 0:NSA_DK] / jnp.maximum(acc_ref[h // stack, rows, NSA_DK:NSA_DK + 1], TINY)
            outs.append(o * _gate_col(g3, h, 1))
        o_ref[0] = jnp.concatenate(outs, axis=1)


def _sel_attn(q, msk, g3, band, new_t, *, t0, tq, tkn, past=None):
    nb, tt, _ = q.shape
    nsp = msk.shape[2] // NSA_GROUPS
    tkp = past[2] if past else tkn
    ppt = tkp // PAGE
    tab, cases = _pair_tables(tt // tq, tq, t0, tkp, tkn)
    stack = NSA_HPG if NSA_HPG * tq <= LANE else 1
    rows = stack * tq
    qtile, const = _flash_specs(tq)
    new_spec = pl.BlockSpec((1, 2 * GD, tkn), lambda b, p, tab, *_: (b, 0, tab[T_J, p] * tab[T_NEW, p]))
    in_specs = [qtile(D_MODEL), qtile(NSA_GROUPS * nsp), qtile(LANE), const(band.shape), new_spec]
    args = [q, msk, g3, band, new_t]
    prefetch = [tab]
    if past:
        page_spec = lambda slot, k: pl.BlockSpec(
            (1, GD, PAGE), lambda b, p, tab, pt: (_past_page(tab, pt, b, p, ppt, k), slot, 0))
        in_specs += [page_spec(s, k) for s in (2, 3) for k in range(ppt)]
        args += [past[1]] * (2 * ppt)
        prefetch.append(past[0])
    return pl.pallas_call(
        functools.partial(_sel_kernel, t0, tq, ppt, past is not None, nsp, cases),
        grid_spec=pltpu.PrefetchScalarGridSpec(
            num_scalar_prefetch=len(prefetch),
            grid=(nb, tab.shape[1]),
            in_specs=in_specs,
            out_specs=qtile(D_MODEL),
            scratch_shapes=[pltpu.VMEM((NSA_HEADS // stack, rows, LANE), F32),
                            pltpu.VMEM((NSA_HEADS // stack, rows, LANE), F32),
                            pltpu.VMEM((NSA_HEADS // stack, rows, NSA_DK + nsp), _tile_dtype(rows))]),
        out_shape=jax.ShapeDtypeStruct((nb, tt, D_MODEL), F32),
        compiler_params=_cparams(("parallel", "arbitrary")),
        name="sel_attn",
    )(*prefetch, *args)


def _win_kernel(t0, tq, n_piece, *refs):
    q_ref, g3_ref, bias_ref = refs[:3]
    piece_refs = refs[3:3 + n_piece]
    o_ref = refs[3 + n_piece]
    cols = [r[0] for r in piece_refs]
    kvt = cols[0] if n_piece == 1 else jnp.concatenate(cols, axis=1)
    nk = kvt.shape[1]
    qpos0 = t0 + pl.program_id(1) * tq
    r = lax.broadcasted_iota(jnp.int32, (tq, nk), 0)
    c = lax.broadcasted_iota(jnp.int32, (tq, nk), 1)
    dw = r - c + WINDOW
    hidden = jnp.where((qpos0 - WINDOW + c >= 0) & (dw >= 0) & (dw < WINDOW), 0.0, MASK_NEG)
    g3 = g3_ref[0]
    outs = []
    for g in range(NSA_GROUPS):
        kt_g = kvt[g * NSA_DK:(g + 1) * NSA_DK, :].astype(BF16)
        vt_g = jnp.concatenate([kvt[GD + g * NSA_DK:GD + (g + 1) * NSA_DK, :].astype(BF16),
                                _unit_block((LANE - NSA_DK, nk), 0)], axis=0)
        for hl in range(NSA_HPG):
            h = g * NSA_HPG + hl
            q_h = q_ref[0, :, h * NSA_DK:(h + 1) * NSA_DK]
            s = jnp.dot(q_h, kt_g, preferred_element_type=F32) + (bias_ref[h] + hidden)
            m = jnp.maximum(jnp.full((tq, LANE), NEG_INF, F32), jnp.max(s, axis=1)[:, None])
            p = jnp.exp(s - _lane_tiles(m, nk // LANE))
            o = _dot_t(p.astype(BF16), vt_g)
            outs.append(o[:, 0:NSA_DK] / jnp.maximum(o[:, NSA_DK:NSA_DK + 1], TINY) * _gate_col(g3, h, 2))
    o_ref[0] = jnp.concatenate(outs, axis=1)


def _win_attn(q, g3, bias, pieces, *, t0, tq):
    nb, tt, _ = q.shape
    tile = lambda width: pl.BlockSpec((1, tq, width), lambda b, i: (b, i, 0))
    in_specs = [tile(D_MODEL), tile(LANE),
                pl.BlockSpec(bias.shape, lambda b, i: (0, 0, 0), pipeline_mode=pl.Buffered(1))]
    for arr, npos, idx in pieces:
        in_specs.append(pl.BlockSpec((1, 2 * GD, npos), functools.partial(lambda idx, b, i: (b, 0, idx(i)), idx)))
    return pl.pallas_call(
        functools.partial(_win_kernel, t0, tq, len(pieces)),
        grid=(nb, tt // tq),
        in_specs=in_specs,
        out_specs=tile(D_MODEL),
        out_shape=jax.ShapeDtypeStruct((nb, tt, D_MODEL), F32),
        compiler_params=_cparams(("parallel", "arbitrary")),
        name="win_attn",
    )(q, g3, bias, *[a for a, _, _ in pieces])


def _window_bias(tab_nsa_t, tq, nk):
    r = np.arange(tq)[:, None]
    c = np.arange(nk)[None, :]
    return _bias_lookup(tab_nsa_t, _t5_bucket_np(r - c + WINDOW)[None]).reshape(NSA_HEADS, tq, nk)


def _diff_kernel(t0, tq, ppt, has_past, lam_init, cases, *refs):
    tab_ref = refs[0]
    refs = refs[2 if has_past else 1:]
    q_ref, lam_ref, sub_ref, band_ref, kvn_ref = refs[:5]
    refs = refs[5:]
    if has_past:
        kvp_refs = refs[0:ppt]
        refs = refs[ppt:]
    o_ref, m_ref, acc_ref, qq_ref = refs
    p = pl.program_id(1)
    _flash_prologue(tab_ref, p, m_ref, acc_ref)

    @pl.when(tab_ref[T_FIRST, p] == 1)
    def _():
        first_map = lax.broadcasted_iota(jnp.int32, (tq, DIFF_DV), 1) < DIFF_DD
        for h in range(DIFF_HEADS):
            q_h = q_ref[0, :, h * DIFF_DV:(h + 1) * DIFF_DV].astype(F32)
            qq_ref[h] = jnp.concatenate([jnp.where(first_map, q_h, 0.0), jnp.where(first_map, 0.0, q_h)],
                                        axis=0).astype(qq_ref.dtype)

    def page_rows(first_row):
        return jnp.concatenate([r[0, pl.ds(first_row, PAGE, stride=2 * DIFF_HEADS), :] for r in kvp_refs],
                               axis=0).astype(BF16)

    fuse = DIFF_HEADS * 2 * tq <= FUSE_ROWS

    def step(is_new, bsel):
        tk = kvn_ref.shape[1] if is_new else ppt * PAGE
        s_list, pv_list = [], []
        for h in range(DIFF_HEADS):
            if is_new:
                k_h = kvn_ref[0, :, h * DIFF_DV:(h + 1) * DIFF_DV]
                v_h = kvn_ref[0, :, D_MODEL + h * DIFF_DV:D_MODEL + (h + 1) * DIFF_DV]
            else:
                k_h, v_h = page_rows(h), page_rows(DIFF_HEADS + h)
            v_h = jnp.concatenate([v_h, _unit_block((tk, LANE), 1)], axis=1)
            s = _dot_t(qq_ref[h].astype(BF16), k_h)
            if bsel != BIAS_FAR:
                extra = _band_correction(band_ref, h, tq, tk, 0 if bsel == BIAS_DIAG else tk)
                if bsel == BIAS_DIAG:
                    extra = extra + _causal_neg(tq, tk)
                s = s + jnp.concatenate([extra, extra], axis=0)
            pv = functools.partial(lambda v, pb: jnp.dot(pb, v, preferred_element_type=F32), v_h)
            if fuse:
                s_list.append(s)
                pv_list.append(pv)
            else:
                _online_softmax_step(s, pv, m_ref.at[h], acc_ref.at[h])
        if fuse:
            _online_softmax_fused(s_list, pv_list, m_ref, acc_ref)

    _when_cases(tab_ref, p, cases, step)

    @pl.when(tab_ref[T_LAST, p] == 1)
    def _():
        lam = lam_ref[0:1, 0:1]
        for h in range(DIFF_HEADS):
            o = acc_ref[h, :, 0:DIFF_DV] / jnp.maximum(acc_ref[h, :, DIFF_DV:DIFF_DV + 1], TINY)
            o = o[0:tq] - lam * o[tq:2 * tq]
            y = o * lax.rsqrt(jnp.mean(o * o, axis=-1, keepdims=True) + EPS) * sub_ref[...]
            o_ref[0, :, h * DIFF_DV:(h + 1) * DIFF_DV] = y * (1.0 - lam_init)


def _diff_attn(q, lam, subln, band, new_rows, *, t0, tq, tkn, lam_init, past=None):
    nb, tt, _ = q.shape
    tkp = past[2] if past else tkn
    ppt = tkp // PAGE
    tab, cases = _pair_tables(tt // tq, tq, t0, tkp, tkn)
    qtile, const = _flash_specs(tq)
    new_spec = pl.BlockSpec((1, tkn, 2 * D_MODEL), lambda b, p, tab, *_: (b, tab[T_J, p] * tab[T_NEW, p], 0))
    in_specs = [qtile(D_MODEL), const(lam.shape), const(subln.shape), const(band.shape), new_spec]
    args = [q, lam, subln, band, new_rows]
    prefetch = [tab]
    if past:
        page_spec = lambda k: pl.BlockSpec(
            (1, PAGE * 2 * DIFF_HEADS, DIFF_DV), lambda b, p, tab, pt: (_past_page(tab, pt, b, p, ppt, k), 0, 0))
        in_specs += [page_spec(k) for k in range(ppt)]
        args += [past[1]] * ppt
        prefetch.append(past[0])
    return pl.pallas_call(
        functools.partial(_diff_kernel, t0, tq, ppt, past is not None, lam_init, cases),
        grid_spec=pltpu.PrefetchScalarGridSpec(
            num_scalar_prefetch=len(prefetch),
            grid=(nb, tab.shape[1]),
            in_specs=in_specs,
            out_specs=qtile(D_MODEL),
            scratch_shapes=[pltpu.VMEM((DIFF_HEADS, 2 * tq, LANE), F32),
                            pltpu.VMEM((DIFF_HEADS, 2 * tq, DIFF_DV + LANE), F32),
                            pltpu.VMEM((DIFF_HEADS, 2 * tq, DIFF_DV), BF16)]),
        out_shape=jax.ShapeDtypeStruct((nb, tt, D_MODEL), F32),
        compiler_params=_cparams(("parallel", "arbitrary")),
        name="diff_attn",
    )(*prefetch, *args)


def _out_mlp_kernel(x_ref, oc_ref, os_ref, ow_ref, od_ref, ga_ref, gb_ref, wo_ref, g_ref, wu_ref, wd_ref, y_ref):
    o_a = oc_ref[...] + os_ref[...] + ow_ref[...]
    o = ga_ref[...] * o_a + gb_ref[...] * od_ref[...]
    h = x_ref[...] + jnp.dot(o.astype(BF16), wo_ref[...], preferred_element_type=F32)
    hn = (h * lax.rsqrt(jnp.mean(h * h, axis=-1, keepdims=True) + EPS) * g_ref[...]).astype(BF16)
    y = h
    for c in range(D_FF // D_MODEL):
        cols = slice(c * D_MODEL, (c + 1) * D_MODEL)
        u = jnp.maximum(jnp.dot(hn, wu_ref[:, cols], preferred_element_type=F32), 0.0)
        y = y + jnp.dot((u * u).astype(BF16), wd_ref[cols, :], preferred_element_type=F32)
    y_ref[...] = y


def _out_mlp(x2d, oc, os_, ow, od, ga, gb, w_o, mlp_g, w_up, w_down, tm):
    m = x2d.shape[0]
    row = pl.BlockSpec((tm, D_MODEL), lambda i: (i, 0))
    const = lambda shape: pl.BlockSpec(shape, lambda i: (0, 0), pipeline_mode=pl.Buffered(1))
    return pl.pallas_call(
        _out_mlp_kernel,
        grid=(m // tm,),
        in_specs=[row] * 7 + [const(w_o.shape), const(mlp_g.shape), const(w_up.shape), const(w_down.shape)],
        out_specs=row,
        out_shape=jax.ShapeDtypeStruct((m, D_MODEL), F32),
        compiler_params=_cparams(("parallel",)),
        name="out_mlp",
    )(x2d, oc, os_, ow, od, ga, gb, w_o, mlp_g, w_up, w_down)


def _win_shift_kernel(n_new, past_ref, new_ref, o_ref):
    keep = o_ref.shape[2]
    o_ref[0] = jnp.concatenate([past_ref[0, :, n_new:keep], new_ref[0, :, 0:n_new]], axis=1)


def _win_shift(past_t, new_t, n_new):
    nb, feats, keep = past_t.shape
    return pl.pallas_call(
        functools.partial(_win_shift_kernel, n_new),
        grid=(nb,),
        in_specs=[pl.BlockSpec((1, feats, keep), lambda b: (b, 0, 0)),
                  pl.BlockSpec((1, feats, new_t.shape[2]), lambda b: (b, 0, 0))],
        out_specs=pl.BlockSpec((1, feats, keep), lambda b: (b, 0, 0)),
        out_shape=jax.ShapeDtypeStruct(past_t.shape, past_t.dtype),
        compiler_params=_cparams(("parallel",)),
        name="win_shift",
    )(past_t, new_t)


TQ_PROMPT = 256
TQ_FLASH = 512
TM_PROMPT = 256
TQ_SAMPLE = 8
TK_PAST = 1024


def _round_up(x, m):
    return -(-x // m) * m


def _feature_major(cache):
    lead = cache.shape[:-4]
    p, s, g, dk = cache.shape[-4:]
    n = len(lead)
    return jnp.transpose(cache, tuple(range(n)) + (n + 1, n + 2, n + 3, n)).reshape(lead + (s * g * dk, p))


def _position_major(feat, slots):
    lead = feat.shape[:-2]
    p = feat.shape[-1]
    n = len(lead)
    x = feat.reshape(lead + (slots, NSA_GROUPS, NSA_DK, p))
    return jnp.transpose(x, tuple(range(n)) + (n + 3, n, n + 1, n + 2))


def _layer_params(l, lam_init, rel_bias_table, attn_norm, w_in, nsa_q_gain, nsa_k_gain,
                  cmp_pe_k, cmp_w1_k, cmp_w2_k, cmp_pe_v, cmp_w1_v, cmp_w2_v,
                  diff_q_gain, diff_k_gain, lambda_q1, lambda_k1, lambda_q2, lambda_k2, diff_subln,
                  w_o, mlp_norm, w_up, w_down):
    tile4 = lambda g: jnp.tile(g.astype(F32), MXU_DIM // NSA_DK)
    gains = jnp.zeros((8, MXU_DIM), F32)
    for r, g in enumerate((nsa_q_gain[l], diff_q_gain[l], diff_k_gain[l])):
        gains = gains.at[r].set(tile4(g))
    flat_pe = lambda pe: jnp.broadcast_to(pe.reshape(1, CMP_BLOCK * NSA_DK), (8, CMP_BLOCK * NSA_DK))
    w1 = jnp.stack([cmp_w1_k[l], cmp_w1_v[l]])
    w2 = jnp.stack([cmp_w2_k[l], cmp_w2_v[l]])
    w2z = jnp.zeros((2, NSA_GROUPS, CMP_HID, GD), F32)
    for g in range(NSA_GROUPS):
        w2z = w2z.at[:, g, :, g * NSA_DK:(g + 1) * NSA_DK].set(w2)
    lam = (jnp.exp(jnp.sum(lambda_q1[l].astype(F32) * lambda_k1[l].astype(F32)))
           - jnp.exp(jnp.sum(lambda_q2[l].astype(F32) * lambda_k2[l].astype(F32))) + lam_init)
    w_row, w_kvt = _split_w_in(w_in[l])
    k_gain = tile4(nsa_k_gain[l])
    return dict(
        attn_g=attn_norm[l].reshape(1, D_MODEL), w_row=w_row, w_kvt=w_kvt, bd=_segment_mean_matrix(),
        gains=gains, k_gain=k_gain.reshape(1, GD), k_gain_col=k_gain.reshape(GD, 1),
        pe2=jnp.stack([flat_pe(cmp_pe_k[l]), flat_pe(cmp_pe_v[l])]), w1=w1, w1_bf=w1.astype(BF16),
        w2z=w2z.astype(BF16),
        tab_nsa_t=rel_bias_table[:, :NSA_HEADS].astype(F32).T, tab_diff_t=rel_bias_table[:, NSA_HEADS:].astype(F32).T,
        lam=jnp.broadcast_to(lam.reshape(1, 1), (8, LANE)), subln=diff_subln[l].reshape(1, DIFF_DV),
        w_o=w_o[l].astype(BF16), mlp_g=mlp_norm[l].reshape(1, D_MODEL),
        w_up=w_up[l].astype(BF16), w_down=w_down[l].astype(BF16))


def _run_proj(P, x, tm):
    kgain_t = jnp.broadcast_to(P['k_gain_col'], (GD, tm))
    return _proj(x, P['attn_g'], P['w_row'], P['w_kvt'], P['bd'], P['gains'], kgain_t, tm)


def _mixers(P, lam_init, q, g3, qd, sel_t, win_pieces, kvd, cmp_pages, tail, *,
            t0, tq, tqf, tkn, seq_len, nsa_past, diff_past):
    nb = q.shape[0]
    total = t0 + seq_len
    h0 = _pe_terms(P['pe2'], P['w1'])
    kc, vc = _compress(cmp_pages[0], cmp_pages[1], nb, cmp_pages[2], tail, P['w1_bf'], h0, P['w2z'], P['bd'],
                       P['k_gain'], page_table=cmp_pages[3])
    ncp = vc.shape[1]
    nc = -(-total // CMP_STRIDE) - CMP_BLOCK // CMP_STRIDE + 1
    n_slc = -(-total // SEL_BLOCK)
    nsp = _round_up(n_slc, LANE)
    bias_c, selwt = _cmp_tables(P['tab_nsa_t'], tqf, ncp, n_slc, nsp)
    o_cmp, msk = _cmp_topk(q, kc, vc, g3, bias_c, selwt, t0=t0, tq=tqf, nc=nc, n_slc=n_slc, n_top=min(N_SEL, n_slc))
    sel_past = dpast = None
    if nsa_past is not None:
        sel_past = (nsa_past[0], nsa_past[1], TK_PAST)
        dpast = (diff_past[0], diff_past[1], TK_PAST)
    o_sel = _sel_attn(q, msk, g3, _band_bias(P['tab_nsa_t']), sel_t, t0=t0, tq=tqf, tkn=tkn, past=sel_past)
    nk = sum(n for _, n, _ in win_pieces)
    o_win = _win_attn(q, g3, _window_bias(P['tab_nsa_t'], tq, nk), win_pieces, t0=t0, tq=tq)
    tkd = kvd.shape[1] if diff_past is not None else tkn
    o_diff = _diff_attn(qd, P['lam'], P['subln'], _band_bias(P['tab_diff_t']), kvd,
                        t0=t0, tq=tqf, tkn=tkd, lam_init=lam_init, past=dpast)
    return o_cmp, o_sel, o_win, o_diff


def _prompt_layer(P, lam_init, x):
    b, t, _ = x.shape
    tq = TQ_PROMPT
    q, nsa_t, win_t, g3, qd, diff5, ga, gb, kvd, sel_t = _run_proj(P, x, TM_PROMPT)
    back = WINDOW // tq
    pieces = [(win_t, tq, functools.partial(lambda k, i: jnp.maximum(i - k, 0), k)) for k in range(back, -1, -1)]
    tail = jnp.zeros((b, 2 * GD, PAGE), F32)
    cmp_pages = (nsa_t, lambda bb, page, pt: (bb, 0, page), t // PAGE, None)
    o_cmp, o_sel, o_win, o_diff = _mixers(
        P, lam_init, q, g3, qd, sel_t, pieces, kvd, cmp_pages, tail,
        t0=0, tq=tq, tqf=min(TQ_FLASH, t), tkn=min(TQ_FLASH, t), seq_len=t, nsa_past=None, diff_past=None)
    f2 = lambda a: a.reshape(b * t, D_MODEL)
    y = _out_mlp(f2(x), f2(o_cmp), f2(o_sel), f2(o_win), f2(o_diff), f2(ga), f2(gb),
                 P['w_o'], P['mlp_g'], P['w_up'], P['w_down'], TM_PROMPT)
    keep = min(WINDOW, t)
    return y.reshape(b, t, D_MODEL), _position_major(nsa_t, 4), diff5, _position_major(win_t[:, :, t - keep:], 2)


def _sample_layer(P, lam_init, x, cache_nsa, cache_diff, cache_win, page_table):
    nb, t, _ = x.shape
    tq = TQ_SAMPLE
    assert t <= tq and cache_win.shape[1] == WINDOW
    past_len = page_table.shape[1] * PAGE
    rows = nb * t
    q, nsa_t, win_t, g3, qd, diff5, ga, gb, kvd, sel_t = _run_proj(P, x.reshape(1, rows, D_MODEL), rows)
    per_seq = lambda a: jnp.pad(a.reshape(nb, t, a.shape[-1]), ((0, 0), (0, tq - t), (0, 0)))
    per_seq_t = lambda a: jnp.pad(jnp.transpose(a.reshape(a.shape[1], nb, t), (1, 0, 2)),
                                  ((0, 0), (0, 0), (0, PAGE - t)))
    nsa_new_t, win_new_t = per_seq_t(nsa_t), per_seq_t(win_t)
    pages_t = _feature_major(cache_nsa)
    past_win_t = _feature_major(cache_win)
    diff_pages = cache_diff.reshape(cache_diff.shape[0], PAGE * 2 * DIFF_HEADS, DIFF_DV)
    pieces = [(past_win_t, WINDOW, lambda i: 0), (win_new_t, PAGE, lambda i: 0)]
    cmp_pages = (pages_t, lambda bb, page, pt: (pt[bb, page], 0, 0), page_table.shape[1], page_table)
    o_cmp, o_sel, o_win, o_diff = _mixers(
        P, lam_init, per_seq(q[0]), per_seq(g3[0]), per_seq(qd[0]), per_seq_t(sel_t), pieces,
        jnp.pad(kvd.reshape(nb, t, 2 * D_MODEL), ((0, 0), (0, PAGE - t), (0, 0))), cmp_pages, nsa_new_t,
        t0=past_len, tq=tq, tqf=tq, tkn=PAGE, seq_len=t,
        nsa_past=(page_table, pages_t), diff_past=(page_table, diff_pages))
    f2 = lambda a: a[:, :t].reshape(rows, D_MODEL)
    y = _out_mlp(x.reshape(rows, D_MODEL), f2(o_cmp), f2(o_sel), f2(o_win), f2(o_diff), ga[0], gb[0],
                 P['w_o'], P['mlp_g'], P['w_up'], P['w_down'], rows)
    win_out_t = _win_shift(past_win_t, win_new_t, t)
    nsa_rows = _position_major(nsa_new_t[:, :, :t], 4)
    return (y.reshape(nb, t, D_MODEL), nsa_rows, diff5.reshape(nb, t, 2, DIFF_HEADS, DIFF_DV),
            _position_major(win_out_t, 2))


def kernel(x_prompt, x_sample, cache_nsa_kv, cache_diff_kv, cache_nsa_win, page_table, rel_bias_table, attn_norm, w_in, nsa_q_gain, nsa_k_gain, cmp_pe_k, cmp_w1_k, cmp_w2_k, cmp_pe_v, cmp_w1_v, cmp_w2_v, diff_q_gain, diff_k_gain, lambda_q1, lambda_k1, lambda_q2, lambda_k2, diff_subln, w_o, mlp_norm, w_up, w_down):
    depth = w_in.shape[0]
    yp, ys = x_prompt, x_sample
    outs = [[] for _ in range(6)]
    for l in range(depth):
        lam_init = 0.8 - 0.6 * math.exp(-0.3 * l)
        P = _layer_params(l, lam_init, rel_bias_table, attn_norm, w_in, nsa_q_gain, nsa_k_gain,
                          cmp_pe_k, cmp_w1_k, cmp_w2_k, cmp_pe_v, cmp_w1_v, cmp_w2_v,
                          diff_q_gain, diff_k_gain, lambda_q1, lambda_k1, lambda_q2, lambda_k2, diff_subln,
                          w_o, mlp_norm, w_up, w_down)
        yp, nsa_p, diff_p, win_p = _prompt_layer(P, lam_init, yp)
        ys, nsa_s, diff_s, win_s = _sample_layer(P, lam_init, ys, cache_nsa_kv[l], cache_diff_kv[l],
                                                 cache_nsa_win[l], page_table)
        for dst, v in zip(outs, (nsa_p, diff_p, win_p, nsa_s, diff_s, win_s)):
            dst.append(v)
    return (yp, ys) + tuple(jnp.stack(o, 0) for o in outs)
```

```python
import functools
import math

import numpy as np
import jax
import jax.numpy as jnp
from jax import lax
from jax.experimental import pallas as pl
from jax.experimental.pallas import tpu as pltpu

F32 = jnp.float32
BF16 = jnp.bfloat16

D_MODEL = 1024
NSA_HEADS = 16
NSA_GROUPS = 4
NSA_HPG = NSA_HEADS // NSA_GROUPS
NSA_DK = D_MODEL // NSA_HEADS
GD = NSA_GROUPS * NSA_DK
CMP_BLOCK = 32
CMP_STRIDE = 16
CMP_HID = 4 * NSA_DK
SEL_BLOCK = 64
N_SEL = 16
WINDOW = 512
FORCE_BONUS = 1e4
DIFF_HEADS = 8
DIFF_DD = D_MODEL // (2 * DIFF_HEADS)
DIFF_DV = 2 * DIFF_DD
D_FF = 4 * D_MODEL
NUM_BUCKETS = 32
MAX_DISTANCE = 128
EPS = 1e-6
NEG_INF = -1e30
TINY = 1e-30
QK_SCALE = NSA_DK ** -0.5

LANE = 128
MXU_DIM = 256
VMEM_LIMIT = 56 * 1024 * 1024
PAGE = 128

C_QN = 0
C_QD = C_QN + D_MODEL
C_KD = C_QD + D_MODEL
C_VD = C_KD + D_MODEL
C_GA = C_VD + D_MODEL
C_GB = C_GA + D_MODEL
C_G3 = C_GB + D_MODEL
N_ROWMAJOR = C_G3 + LANE


def _cparams(sem):
    return pltpu.CompilerParams(dimension_semantics=sem, vmem_limit_bytes=VMEM_LIMIT)


def _t5_bucket_np(dist):
    n = np.maximum(dist, 0)
    max_exact = NUM_BUCKETS // 2
    nf = np.maximum(n, 1).astype(np.float32)
    log_ratio = np.log(nf / np.float32(max_exact)) / np.float32(math.log(MAX_DISTANCE / max_exact))
    large = max_exact + (log_ratio * np.float32(NUM_BUCKETS - max_exact)).astype(np.int32)
    large = np.minimum(large, NUM_BUCKETS - 1)
    return np.where(n < max_exact, n, large).astype(np.int32)


def _segment_mean_matrix():
    i = np.arange(MXU_DIM)
    return jnp.asarray((i[:, None] // NSA_DK == i[None, :] // NSA_DK) / NSA_DK, dtype=BF16)


def _dot_t(a, b, precision=None):
    return lax.dot_general(a, b, (((1,), (1,)), ((), ())), preferred_element_type=F32, precision=precision)


def _bias_lookup_kernel(hp, tab_ref, idx_ref, o_ref):
    n, r, c = idx_ref.shape
    for k in range(n):
        idx = idx_ref[k]
        for hh in range(hp):
            h = pl.program_id(0) * hp + hh
            acc = jnp.zeros((r, c), F32)
            for b in range(NUM_BUCKETS):
                acc = jnp.where(idx == b, tab_ref[h, b], acc)
            o_ref[0, k, hh * r:(hh + 1) * r, :] = acc


def _bias_lookup(tab_t, idx, hp=1):
    heads = tab_t.shape[0]
    n, r, c = idx.shape
    return pl.pallas_call(
        functools.partial(_bias_lookup_kernel, hp),
        grid=(heads // hp,),
        in_specs=[pl.BlockSpec(memory_space=pltpu.SMEM),
                  pl.BlockSpec((n, r, c), lambda g: (0, 0, 0), pipeline_mode=pl.Buffered(1))],
        out_specs=pl.BlockSpec((1, n, hp * r, c), lambda g: (g, 0, 0, 0)),
        out_shape=jax.ShapeDtypeStruct((heads // hp, n, hp * r, c), F32),
        compiler_params=_cparams(("parallel",)),
        name="bias_lookup",
    )(tab_t, jnp.asarray(idx, dtype=jnp.int32))


def _proj_kernel(x_ref, g_ref, w_ref, wt_ref, bd_ref, gains_ref, kgt_ref,
                 q_ref, nsat_ref, wint_ref, g3_ref, qd_ref, diff_ref, ga_ref, gb_ref, kvd_ref, selt_ref):
    x = x_ref[0]
    xn = x * lax.rsqrt(jnp.mean(x * x, axis=-1, keepdims=True) + EPS) * g_ref[...]
    xn = xn.astype(BF16)
    bd = bd_ref[...]

    def mm(c0, width=MXU_DIM):
        return jnp.dot(xn, w_ref[:, c0:c0 + width], preferred_element_type=F32)

    def headnorm(acc, gain_row):
        ms = jnp.dot((acc * acc).astype(BF16), bd, preferred_element_type=F32)
        return acc * lax.rsqrt(ms + EPS) * gains_ref[gain_row:gain_row + 1, :]

    heads_per_tile = MXU_DIM // DIFF_DV
    for c in range(D_MODEL // MXU_DIM):
        o = c * MXU_DIM
        q_ref[0, :, o:o + MXU_DIM] = (headnorm(mm(C_QN + o), 0) * QK_SCALE).astype(BF16)
        qd_ref[0, :, o:o + MXU_DIM] = (headnorm(mm(C_QD + o), 1) * QK_SCALE).astype(BF16)
        kd = headnorm(mm(C_KD + o), 2)
        vd = mm(C_VD + o)
        for hh in range(heads_per_tile):
            h = c * heads_per_tile + hh
            diff_ref[0, :, 0, h, :] = kd[:, hh * DIFF_DV:(hh + 1) * DIFF_DV]
            diff_ref[0, :, 1, h, :] = vd[:, hh * DIFF_DV:(hh + 1) * DIFF_DV]
        kvd_ref[0, :, o:o + MXU_DIM] = kd.astype(BF16)
        kvd_ref[0, :, D_MODEL + o:D_MODEL + o + MXU_DIM] = vd.astype(BF16)
        ga_ref[0, :, o:o + MXU_DIM] = jax.nn.sigmoid(mm(C_GA + o))
        gb_ref[0, :, o:o + MXU_DIM] = jax.nn.sigmoid(mm(C_GB + o))
    g3_ref[0] = jax.nn.sigmoid(mm(C_G3, LANE))

    def mm_t(slot):
        return _dot_t(wt_ref[slot * GD:(slot + 1) * GD, :], xn)

    def headnorm_t(acc):
        ms = jnp.dot(bd, (acc * acc).astype(BF16), preferred_element_type=F32)
        return acc * lax.rsqrt(ms + EPS) * kgt_ref[...]

    nsat_ref[0, 0:GD, :] = mm_t(0)
    nsat_ref[0, GD:2 * GD, :] = mm_t(1)
    k_sel, v_sel = headnorm_t(mm_t(2)), mm_t(3)
    nsat_ref[0, 2 * GD:3 * GD, :] = k_sel
    nsat_ref[0, 3 * GD:4 * GD, :] = v_sel
    selt_ref[0, 0:GD, :] = k_sel.astype(BF16)
    selt_ref[0, GD:2 * GD, :] = v_sel.astype(BF16)
    wint_ref[0, 0:GD, :] = headnorm_t(mm_t(4))
    wint_ref[0, GD:2 * GD, :] = mm_t(5)


def _proj(x, attn_g, w_row, w_kvt, bd, gains, kgain_t, tm):
    nb, t, _ = x.shape
    row = lambda width: pl.BlockSpec((1, tm, width), lambda b, i: (b, i, 0))
    col = lambda feats: pl.BlockSpec((1, feats, tm), lambda b, i: (b, 0, i))
    const = lambda shape: pl.BlockSpec(shape, lambda b, i: (0, 0), pipeline_mode=pl.Buffered(1))
    out_specs = (row(D_MODEL), col(4 * GD), col(2 * GD), row(LANE), row(D_MODEL),
                 pl.BlockSpec((1, tm, 2, DIFF_HEADS, DIFF_DV), lambda b, i: (b, i, 0, 0, 0)),
                 row(D_MODEL), row(D_MODEL), row(2 * D_MODEL), col(2 * GD))
    out_shapes = (
        jax.ShapeDtypeStruct((nb, t, D_MODEL), BF16),
        jax.ShapeDtypeStruct((nb, 4 * GD, t), F32),
        jax.ShapeDtypeStruct((nb, 2 * GD, t), F32),
        jax.ShapeDtypeStruct((nb, t, LANE), F32),
        jax.ShapeDtypeStruct((nb, t, D_MODEL), BF16),
        jax.ShapeDtypeStruct((nb, t, 2, DIFF_HEADS, DIFF_DV), F32),
        jax.ShapeDtypeStruct((nb, t, D_MODEL), F32),
        jax.ShapeDtypeStruct((nb, t, D_MODEL), F32),
        jax.ShapeDtypeStruct((nb, t, 2 * D_MODEL), BF16),
        jax.ShapeDtypeStruct((nb, 2 * GD, t), BF16),
    )
    return pl.pallas_call(
        _proj_kernel,
        grid=(nb, t // tm),
        in_specs=[row(D_MODEL), const((1, D_MODEL)), const(w_row.shape), const(w_kvt.shape),
                  const(bd.shape), const(gains.shape), const(kgain_t.shape)],
        out_specs=out_specs,
        out_shape=out_shapes,
        compiler_params=_cparams(("parallel", "parallel")),
        name="proj",
    )(x, attn_g, w_row, w_kvt, bd, gains, kgain_t)


def _split_w_in(w_in):
    sizes = (D_MODEL, 6 * GD, 3 * NSA_HEADS, D_MODEL, D_MODEL, D_MODEL, D_MODEL, D_MODEL)
    pts = np.cumsum(sizes)[:-1]
    q_n, kv_n, g_n, q_d, k_d, v_d, ga, gb = jnp.split(w_in, pts, axis=-1)
    g_pad = jnp.pad(g_n, ((0, 0), (0, LANE - 3 * NSA_HEADS)))
    w_row = jnp.concatenate([q_n, q_d, k_d, v_d, ga, gb, g_pad], axis=-1).astype(BF16)
    return w_row, kv_n.T.astype(BF16)


CH_PER_PAGE = PAGE // CMP_STRIDE
J_PER_TILE = MXU_DIM // NSA_DK


def _pe_kernel(pe_ref, w1_ref, o_ref):
    for kv in range(2):
        o_ref[kv] = jnp.dot(pe_ref[kv], w1_ref[kv], preferred_element_type=F32,
                            precision=lax.Precision.HIGHEST)


def _pe_terms(pe2, w1):
    return pl.pallas_call(
        _pe_kernel,
        out_shape=jax.ShapeDtypeStruct((2, 8, CMP_HID), F32),
        name="cmp_pe",
    )(pe2, w1)


def _compress_kernel(pps, *refs):
    page_refs = refs[1:1 + pps]
    next_ref, tail_ref, w1_ref, h0_ref, w2_ref, bd_ref, kg_ref, kc_ref, vc_ref = refs[1 + pps:10 + pps]
    rows_refs = refs[10 + pps:]
    is_last = pl.program_id(1) == pl.num_programs(1) - 1
    cb = pps * CH_PER_PAGE
    rows = cb + CH_PER_PAGE
    gpl = LANE // NSA_DK
    for fb, rows_ref in enumerate(rows_refs):
        feats = slice(fb * LANE, (fb + 1) * LANE)
        for k, p in enumerate(page_refs):
            rows_ref[k * PAGE:(k + 1) * PAGE, :] = p[0, feats, :].T
        rows_ref[pps * PAGE:(pps + 1) * PAGE, :] = jnp.where(is_last, tail_ref[0, feats, :], next_ref[0, feats, :]).T

    for kv in range(2):
        acc = [None, None]
        for jt in range(CMP_STRIDE // J_PER_TILE):
            xs = [[rows_refs[kv * (GD // LANE) + half][pl.ds(jt * J_PER_TILE + jj, rows, stride=CMP_STRIDE), :]
                   for half in range(GD // LANE)] for jj in range(J_PER_TILE)]
            lhs = jnp.concatenate(
                [jnp.concatenate([x[g // gpl][:, (g % gpl) * NSA_DK:(g % gpl + 1) * NSA_DK] for x in xs], axis=1)
                 for g in range(NSA_GROUPS)], axis=0).astype(BF16)
            for r in range(2):
                k0 = r * CMP_STRIDE * NSA_DK + jt * MXU_DIM
                part = jnp.dot(lhs, w1_ref[kv, k0:k0 + MXU_DIM, :], preferred_element_type=F32)
                acc[r] = part if acc[r] is None else acc[r] + part
        out = None
        for g in range(NSA_GROUPS):
            h = h0_ref[kv, 0:1, :] + acc[0][g * rows:g * rows + cb] + acc[1][g * rows + 1:g * rows + 1 + cb]
            part = jnp.dot(jax.nn.gelu(h).astype(BF16), w2_ref[kv, g], preferred_element_type=F32)
            out = part if out is None else out + part
        if kv == 0:
            ms = jnp.dot((out * out).astype(BF16), bd_ref[...], preferred_element_type=F32)
            kc_ref[0] = (out * lax.rsqrt(ms + EPS) * kg_ref[...]).T
        else:
            vc_ref[0] = out


def _compress(pages, page_index, nb, npg, tail, w1, h0, w2z, bd, kgain, page_table=None):
    pps = min(16, npg)
    assert npg % pps == 0
    prefetch = [page_table if page_table is not None else jnp.zeros((1, 1), jnp.int32)]
    blk = (1, 2 * GD, PAGE)
    page_spec = lambda k: pl.BlockSpec(blk, lambda b, s, pt: page_index(b, s * pps + k, pt))
    next_spec = pl.BlockSpec(blk, lambda b, s, pt: page_index(b, jnp.minimum(s * pps + pps, npg - 1), pt))
    const = lambda shape: pl.BlockSpec(shape, lambda b, s, pt: (0,) * len(shape), pipeline_mode=pl.Buffered(1))
    out_spec = pl.BlockSpec((1, pps * CH_PER_PAGE, GD), lambda b, s, pt: (b, s, 0))
    out_sds = jax.ShapeDtypeStruct((nb, npg * CH_PER_PAGE, GD), F32)
    return pl.pallas_call(
        functools.partial(_compress_kernel, pps),
        grid_spec=pltpu.PrefetchScalarGridSpec(
            num_scalar_prefetch=1,
            grid=(nb, npg // pps),
            in_specs=[page_spec(k) for k in range(pps)] + [
                next_spec,
                pl.BlockSpec(blk, lambda b, s, pt: (b, 0, 0)),
                const(w1.shape), const(h0.shape), const(w2z.shape), const(bd.shape), const(kgain.shape)],
            out_specs=(pl.BlockSpec((1, GD, pps * CH_PER_PAGE), lambda b, s, pt: (b, 0, s)), out_spec),
            scratch_shapes=[pltpu.VMEM(((pps + 1) * PAGE, LANE), F32)] * (2 * GD // LANE)),
        out_shape=(jax.ShapeDtypeStruct((nb, GD, npg * CH_PER_PAGE), F32), out_sds),
        compiler_params=_cparams(("parallel", "arbitrary")),
        name="compress",
    )(*prefetch, *([pages] * (pps + 1)), tail, w1, h0, w2z, bd, kgain)


CMP_WIN = 64
CMP_WIN_BACK = 32
SCORE_PAD = -(2.0 ** 126)


def _gate_col(g3, head, branch):
    c = head * 3 + branch
    return g3[:, c:c + 1]


def _cmp_topk_kernel(t0, tq, ncp, nc, n_slc, nsp, n_top,
                     q_ref, kct_ref, vc_ref, g3_ref, bias_ref, selwt_ref, o_ref, msk_ref):
    qpos0 = t0 + pl.program_id(1) * tq
    col = lax.broadcasted_iota(jnp.int32, (tq, ncp), 1)
    qpos = qpos0 + lax.broadcasted_iota(jnp.int32, (tq, ncp), 0)
    hidden = jnp.where((col * CMP_STRIDE + (CMP_BLOCK - 1) <= qpos) & (col < nc), 0.0, MASK_NEG)
    any_visible = qpos0 + lax.broadcasted_iota(jnp.int32, (tq, 1), 0) >= CMP_BLOCK - 1
    c0 = qpos0 // CMP_STRIDE - CMP_WIN_BACK
    place = (lax.broadcasted_iota(jnp.int32, (CMP_WIN, ncp), 1)
             == lax.broadcasted_iota(jnp.int32, (CMP_WIN, ncp), 0) + c0).astype(BF16)
    g3 = g3_ref[0]
    n_lanes = _round_up(NSA_GROUPS * tq, LANE)
    blk = lax.broadcasted_iota(jnp.int32, (nsp, n_lanes), 0)
    bpos = qpos0 + lax.broadcasted_iota(jnp.int32, (nsp, n_lanes), 1) % tq
    cur = bpos // SEL_BLOCK
    forced = (blk == 0) | (blk == cur) | (blk == cur - 1)
    valid_b = blk * SEL_BLOCK <= bpos
    outs, psums = [], []
    for g in range(NSA_GROUPS):
        kc_g = jnp.concatenate([place, place, kct_ref[0, g * NSA_DK:(g + 1) * NSA_DK, :].astype(BF16)], axis=0)
        vc_g = vc_ref[0, :, g * NSA_DK:(g + 1) * NSA_DK].astype(BF16)
        psum = jnp.zeros((tq, ncp), F32)
        for hl in range(NSA_HPG):
            h = g * NSA_HPG + hl
            q_h = jnp.concatenate([bias_ref[h], q_ref[0, :, h * NSA_DK:(h + 1) * NSA_DK]], axis=1)
            s = jnp.dot(q_h, kc_g, preferred_element_type=F32) + hidden
            p = jnp.exp(s - jnp.max(s, axis=1)[:, None])
            inv = jnp.where(any_visible, 1.0 / jnp.maximum(jnp.sum(p, axis=1)[:, None], TINY), 0.0)
            p = p * inv
            o_h = jnp.dot(p.astype(BF16), vc_g, preferred_element_type=F32)
            outs.append(o_h * _gate_col(g3, h, 0))
            psum = psum + p
        psums.append(psum)
    if n_lanes != NSA_GROUPS * tq:
        psums.append(jnp.zeros((n_lanes - NSA_GROUPS * tq, ncp), F32))
    imp_t = _dot_t(selwt_ref[...], jnp.concatenate(psums, axis=0), precision=lax.Precision.HIGHEST)
    score = jnp.where(valid_b, imp_t + jnp.where(forced, FORCE_BONUS, 0.0), -FORCE_BONUS)
    score = jnp.where(blk < n_slc, score, SCORE_PAD)

    def pick(_, sc):
        best = jnp.max(sc, axis=0, keepdims=True)
        first = jnp.min(jnp.where(sc == best, blk, nsp), axis=0, keepdims=True)
        return jnp.where(blk == first, SCORE_PAD, sc)

    picked = lax.fori_loop(0, n_top, pick, score)
    hidden_blocks = jnp.where((picked < 0.5 * SCORE_PAD) & (blk < n_slc), 0.0, MASK_NEG).T
    for g in range(NSA_GROUPS):
        msk_ref[0, :, g * nsp:(g + 1) * nsp] = hidden_blocks[g * tq:(g + 1) * tq].astype(BF16)
    o_ref[0] = jnp.concatenate(outs, axis=1)


def _cmp_topk(q, kct, vc, g3, bias, selwt, *, t0, tq, nc, n_slc, n_top):
    nb, tt, _ = q.shape
    ncp = vc.shape[1]
    nsp = selwt.shape[0]
    const = lambda shape: pl.BlockSpec(shape, lambda b, i: (0,) * len(shape), pipeline_mode=pl.Buffered(1))
    per_b = lambda shape: pl.BlockSpec(shape, lambda b, i: (b, 0, 0))
    tile = lambda width: pl.BlockSpec((1, tq, width), lambda b, i: (b, i, 0))
    return pl.pallas_call(
        functools.partial(_cmp_topk_kernel, t0, tq, ncp, nc, n_slc, nsp, n_top),
        grid=(nb, tt // tq),
        in_specs=[tile(D_MODEL), per_b((1, GD, ncp)), per_b((1, ncp, GD)), tile(LANE),
                  const(bias.shape), const(selwt.shape)],
        out_specs=(tile(D_MODEL), tile(NSA_GROUPS * nsp)),
        out_shape=(jax.ShapeDtypeStruct((nb, tt, D_MODEL), F32),
                   jax.ShapeDtypeStruct((nb, tt, NSA_GROUPS * nsp), BF16)),
        compiler_params=_cparams(("parallel", "arbitrary")),
        name="cmp_topk",
    )(q, kct, vc, g3, bias, selwt)


def _cmp_tables(tab_nsa_t, tq, ncp, n_slc, nsp):
    r = np.arange(tq)[:, None]
    w = np.arange(CMP_WIN)[None, :]
    dc = r - CMP_STRIDE * (w - CMP_WIN_BACK) - (CMP_BLOCK - 1)
    idx = np.where(dc >= 0, _t5_bucket_np(dc), NUM_BUCKETS - 1)
    win = _bias_lookup(tab_nsa_t, idx[None])[:, 0] - tab_nsa_t[:, NUM_BUCKETS - 1].reshape(-1, 1, 1)
    high = win.astype(BF16)
    bias = jnp.concatenate([high, (win - high.astype(F32)).astype(BF16)], axis=-1)
    c_lo = np.arange(ncp)[None, :] * CMP_STRIDE
    s_lo = np.arange(nsp)[:, None] * SEL_BLOCK
    shared = np.clip(np.minimum(c_lo + CMP_BLOCK, s_lo + SEL_BLOCK) - np.maximum(c_lo, s_lo), 0, None)
    shared = np.where(np.arange(nsp)[:, None] < n_slc, shared, 0)
    return bias, jnp.asarray(shared / CMP_BLOCK, dtype=F32)


BIAS_DIAG, BIAS_PREV, BIAS_FAR = 0, 1, 2
T_I, T_J, T_FIRST, T_LAST, T_BIAS, T_NEW, T_KPOS = range(7)


def _pair_tables(nq, tq, t0, tkp, tkn):
    assert t0 % tkp == 0
    n_past = t0 // tkp
    rows = []
    for i in range(nq):
        q0 = t0 + i * tq
        tiles = [(0, j, j * tkp, tkp) for j in range(n_past)]
        tiles += [(1, j, t0 + j * tkn, tkn) for j in range((i * tq + tq - 1) // tkn + 1)]
        for n, (is_new, j, k0, tk) in enumerate(tiles):
            delta = q0 - k0
            assert delta in (0, tk) or delta >= tk + MAX_DISTANCE - 1, (tq, t0, tkp, tkn)
            bsel = BIAS_DIAG if delta == 0 else BIAS_PREV if delta == tk else BIAS_FAR
            rows.append((i, j, int(n == 0), int(n == len(tiles) - 1), bsel, is_new, k0))
    tab = np.array(rows, dtype=np.int32).T.copy()
    cases = sorted({(int(r[T_NEW]), int(r[T_BIAS])) for r in tab.T})
    return jnp.asarray(tab), tuple(cases)


BAND = MAX_DISTANCE


def _band_bias(tab_t):
    r = np.arange(BAND)[:, None]
    c = np.arange(BAND)[None, :]
    tiles = _bias_lookup(tab_t, np.stack([_t5_bucket_np(r - c), _t5_bucket_np(r - c + BAND)]))
    return tiles - tab_t[:, NUM_BUCKETS - 1].reshape(-1, 1, 1, 1)


def _when_cases(tab_ref, p, cases, fn):
    for is_new, bsel in cases:
        pl.when((tab_ref[T_NEW, p] == is_new) & (tab_ref[T_BIAS, p] == bsel))(functools.partial(fn, is_new, bsel))


MASK_NEG = -1e30


def _tile_dtype(rows):
    return BF16 if rows % 16 == 0 else F32


def _band_correction(band_ref, head, tq, tk, delta):
    rb_rows, cb_cols = min(tq, BAND), min(tk, BAND)
    rows = []
    for rb in range(max(tq // BAND, 1)):
        cols = []
        for cb in range(max(tk // BAND, 1)):
            d = delta // BAND + rb - cb
            cols.append(band_ref[head, d, 0:rb_rows, 0:cb_cols] if d in (0, 1) else jnp.zeros((rb_rows, cb_cols), F32))
        rows.append(cols[0] if len(cols) == 1 else jnp.concatenate(cols, axis=1))
    return rows[0] if len(rows) == 1 else jnp.concatenate(rows, axis=0)


def _lane_tiles(x, n):
    return x if n == 1 else jnp.concatenate([x] * n, axis=1)


def _online_softmax_step(s, pv, m_ref, acc_ref):
    m_prev = m_ref[...]
    m_next = jnp.maximum(m_prev, jnp.max(s, axis=1)[:, None])
    p = jnp.exp(s - _lane_tiles(m_next, s.shape[1] // LANE))
    alpha = jnp.exp(m_prev - m_next)
    m_ref[...] = m_next
    acc_ref[...] = _lane_tiles(alpha, acc_ref.shape[1] // LANE) * acc_ref[...] + pv(p.astype(BF16))


def _online_softmax_fused(s_list, pv_list, m_ref, acc_ref):
    n, rows = len(s_list), s_list[0].shape[0]
    width = acc_ref.shape[2]
    s = jnp.concatenate(s_list, axis=0)
    m_prev = m_ref[...].reshape(n * rows, LANE)
    m_next = jnp.maximum(m_prev, jnp.max(s, axis=1)[:, None])
    p = jnp.exp(s - _lane_tiles(m_next, s.shape[1] // LANE)).astype(BF16)
    alpha = jnp.exp(m_prev - m_next)
    m_ref[...] = m_next.reshape(n, rows, LANE)
    pv = jnp.concatenate([pv_list[k](p[k * rows:(k + 1) * rows]) for k in range(n)], axis=0)
    acc = _lane_tiles(alpha, width // LANE) * acc_ref[...].reshape(n * rows, width) + pv
    acc_ref[...] = acc.reshape(n, rows, width)


FUSE_ROWS = 128


def _unit_block(shape, axis):
    return jnp.where(lax.broadcasted_iota(jnp.int32, shape, axis) == 0, 1.0, 0.0).astype(BF16)


def _causal_neg(tq, tk):
    r = lax.broadcasted_iota(jnp.int32, (tq, tk), 0)
    c = lax.broadcasted_iota(jnp.int32, (tq, tk), 1)
    return jnp.where(c <= r, 0.0, MASK_NEG).astype(F32)


def _flash_prologue(tab_ref, p, m_ref, acc_ref):
    @pl.when(tab_ref[T_FIRST, p] == 1)
    def _():
        m_ref[...] = jnp.full(m_ref.shape, NEG_INF, F32)
        acc_ref[...] = jnp.zeros(acc_ref.shape, F32)


def _flash_specs(tq):
    qtile = lambda width: pl.BlockSpec((1, tq, width), lambda b, p, tab, *_: (b, tab[T_I, p], 0))
    const = lambda shape: pl.BlockSpec(shape, lambda b, p, tab, *_: (0,) * len(shape), pipeline_mode=pl.Buffered(1))
    return qtile, const


def _past_page(tab, pt, b, p, ppt, k):
    n_past = pt.shape[1] // ppt
    j = jnp.where(tab[T_NEW, p] == 1, n_past - 1, tab[T_J, p])
    return pt[b, j * ppt + k]


def _sel_kernel(t0, tq, ppt, has_past, nsp, cases, *refs):
    tab_ref = refs[0]
    refs = refs[2 if has_past else 1:]
    q_ref, msk_ref, g3_ref, band_ref, kvn_ref = refs[:5]
    refs = refs[5:]
    if has_past:
        kp_refs, vp_refs = refs[0:ppt], refs[ppt:2 * ppt]
        refs = refs[2 * ppt:]
    o_ref, m_ref, acc_ref, qh_ref = refs
    p = pl.program_id(1)
    kpos0 = tab_ref[T_KPOS, p]
    _flash_prologue(tab_ref, p, m_ref, acc_ref)
    stack = qh_ref.shape[1] // tq
    fuse = NSA_HEADS * tq <= FUSE_ROWS

    @pl.when(tab_ref[T_FIRST, p] == 1)
    def _():
        for hs in range(NSA_HEADS // stack):
            g = hs * stack // NSA_HPG
            q_hs = jnp.concatenate(
                [q_ref[0, :, h * NSA_DK:(h + 1) * NSA_DK] for h in range(hs * stack, (hs + 1) * stack)], axis=0)
            msk_g = msk_ref[0, :, g * nsp:(g + 1) * nsp]
            msk_hs = msk_g if stack == 1 else jnp.concatenate([msk_g] * stack, axis=0)
            qh_ref[hs] = jnp.concatenate([q_hs, msk_hs], axis=1).astype(qh_ref.dtype)

    def step(is_new, bsel):
        if is_new:
            kt_all, vt_all = kvn_ref[0, 0:GD, :], kvn_ref[0, GD:2 * GD, :]
        else:
            kt_all = jnp.concatenate([r[0] for r in kp_refs], axis=1)
            vt_all = jnp.concatenate([r[0] for r in vp_refs], axis=1)
        tk = kt_all.shape[1]
        blk_of_key = (kpos0 + lax.broadcasted_iota(jnp.int32, (nsp, tk), 1)) // SEL_BLOCK
        expand = (lax.broadcasted_iota(jnp.int32, (nsp, tk), 0) == blk_of_key).astype(BF16)
        s_list, pv_list = [], []
        causal = _causal_neg(tq, tk) if bsel == BIAS_DIAG else None
        for g in range(NSA_GROUPS):
            kt_g = jnp.concatenate([kt_all[g * NSA_DK:(g + 1) * NSA_DK, :].astype(BF16), expand], axis=0)
            vt_g = jnp.concatenate([vt_all[g * NSA_DK:(g + 1) * NSA_DK, :].astype(BF16),
                                    _unit_block((LANE - NSA_DK, tk), 0)], axis=0)
            for hs in range(g * NSA_HPG // stack, (g + 1) * NSA_HPG // stack):
                s = jnp.dot(qh_ref[hs].astype(BF16), kt_g, preferred_element_type=F32)
                if bsel != BIAS_FAR:
                    extra = [_band_correction(band_ref, h, tq, tk, 0 if bsel == BIAS_DIAG else tk)
                             for h in range(hs * stack, (hs + 1) * stack)]
                    if causal is not None:
                        extra = [e + causal for e in extra]
                    s = s + (extra[0] if stack == 1 else jnp.concatenate(extra, axis=0))
                pv = functools.partial(lambda vt, pb: _dot_t(pb, vt), vt_g)
                if fuse:
                    s_list.append(s)
                    pv_list.append(pv)
                else:
                    _online_softmax_step(s, pv, m_ref.at[hs], acc_ref.at[hs])
        if fuse:
            _online_softmax_fused(s_list, pv_list, m_ref, acc_ref)

    _when_cases(tab_ref, p, cases, step)

    @pl.when(tab_ref[T_LAST, p] == 1)
    def _():
        g3 = g3_ref[0]
        outs = []
        for h in range(NSA_HEADS):
            rows = slice((h % stack) * tq, (h % stack + 1) * tq)
            o = acc_ref[h // stack, rows, 0:NSA_DK] / jnp.maximum(acc_ref[h // stack, rows, NSA_DK:NSA_DK + 1], TINY)
            outs.append(o * _gate_col(g3, h, 1))
        o_ref[0] = jnp.concatenate(outs, axis=1)


def _sel_attn(q, msk, g3, band, new_t, *, t0, tq, tkn, past=None):
    nb, tt, _ = q.shape
    nsp = msk.shape[2] // NSA_GROUPS
    tkp = past[2] if past else tkn
    ppt = tkp // PAGE
    tab, cases = _pair_tables(tt // tq, tq, t0, tkp, tkn)
    stack = NSA_HPG if NSA_HPG * tq <= LANE else 1
    rows = stack * tq
    qtile, const = _flash_specs(tq)
    new_spec = pl.BlockSpec((1, 2 * GD, tkn), lambda b, p, tab, *_: (b, 0, tab[T_J, p] * tab[T_NEW, p]))
    in_specs = [qtile(D_MODEL), qtile(NSA_GROUPS * nsp), qtile(LANE), const(band.shape), new_spec]
    args = [q, msk, g3, band, new_t]
    prefetch = [tab]
    if past:
        page_spec = lambda slot, k: pl.BlockSpec(
            (1, GD, PAGE), lambda b, p, tab, pt: (_past_page(tab, pt, b, p, ppt, k), slot, 0))
        in_specs += [page_spec(s, k) for s in (2, 3) for k in range(ppt)]
        args += [past[1]] * (2 * ppt)
        prefetch.append(past[0])
    return pl.pallas_call(
        functools.partial(_sel_kernel, t0, tq, ppt, past is not None, nsp, cases),
        grid_spec=pltpu.PrefetchScalarGridSpec(
            num_scalar_prefetch=len(prefetch),
            grid=(nb, tab.shape[1]),
            in_specs=in_specs,
            out_specs=qtile(D_MODEL),
            scratch_shapes=[pltpu.VMEM((NSA_HEADS // stack, rows, LANE), F32),
                            pltpu.VMEM((NSA_HEADS // stack, rows, LANE), F32),
                            pltpu.VMEM((NSA_HEADS // stack, rows, NSA_DK + nsp), _tile_dtype(rows))]),
        out_shape=jax.ShapeDtypeStruct((nb, tt, D_MODEL), F32),
        compiler_params=_cparams(("parallel", "arbitrary")),
        name="sel_attn",
    )(*prefetch, *args)


def _win_kernel(t0, tq, n_piece, *refs):
    q_ref, g3_ref, bias_ref = refs[:3]
    piece_refs = refs[3:3 + n_piece]
    o_ref = refs[3 + n_piece]
    cols = [r[0] for r in piece_refs]
    kvt = cols[0] if n_piece == 1 else jnp.concatenate(cols, axis=1)
    nk = kvt.shape[1]
    qpos0 = t0 + pl.program_id(1) * tq
    r = lax.broadcasted_iota(jnp.int32, (tq, nk), 0)
    c = lax.broadcasted_iota(jnp.int32, (tq, nk), 1)
    dw = r - c + WINDOW
    hidden = jnp.where((qpos0 - WINDOW + c >= 0) & (dw >= 0) & (dw < WINDOW), 0.0, MASK_NEG)
    g3 = g3_ref[0]
    outs = []
    for g in range(NSA_GROUPS):
        kt_g = kvt[g * NSA_DK:(g + 1) * NSA_DK, :].astype(BF16)
        vt_g = jnp.concatenate([kvt[GD + g * NSA_DK:GD + (g + 1) * NSA_DK, :].astype(BF16),
                                _unit_block((LANE - NSA_DK, nk), 0)], axis=0)
        for hl in range(NSA_HPG):
            h = g * NSA_HPG + hl
            q_h = q_ref[0, :, h * NSA_DK:(h + 1) * NSA_DK]
            s = jnp.dot(q_h, kt_g, preferred_element_type=F32) + (bias_ref[h] + hidden)
            m = jnp.maximum(jnp.full((tq, LANE), NEG_INF, F32), jnp.max(s, axis=1)[:, None])
            p = jnp.exp(s - _lane_tiles(m, nk // LANE))
            o = _dot_t(p.astype(BF16), vt_g)
            outs.append(o[:, 0:NSA_DK] / jnp.maximum(o[:, NSA_DK:NSA_DK + 1], TINY) * _gate_col(g3, h, 2))
    o_ref[0] = jnp.concatenate(outs, axis=1)


def _win_attn(q, g3, bias, pieces, *, t0, tq):
    nb, tt, _ = q.shape
    tile = lambda width: pl.BlockSpec((1, tq, width), lambda b, i: (b, i, 0))
    in_specs = [tile(D_MODEL), tile(LANE),
                pl.BlockSpec(bias.shape, lambda b, i: (0, 0, 0), pipeline_mode=pl.Buffered(1))]
    for arr, npos, idx in pieces:
        in_specs.append(pl.BlockSpec((1, 2 * GD, npos), functools.partial(lambda idx, b, i: (b, 0, idx(i)), idx)))
    return pl.pallas_call(
        functools.partial(_win_kernel, t0, tq, len(pieces)),
        grid=(nb, tt // tq),
        in_specs=in_specs,
        out_specs=tile(D_MODEL),
        out_shape=jax.ShapeDtypeStruct((nb, tt, D_MODEL), F32),
        compiler_params=_cparams(("parallel", "arbitrary")),
        name="win_attn",
    )(q, g3, bias, *[a for a, _, _ in pieces])


def _window_bias(tab_nsa_t, tq, nk):
    r = np.arange(tq)[:, None]
    c = np.arange(nk)[None, :]
    return _bias_lookup(tab_nsa_t, _t5_bucket_np(r - c + WINDOW)[None]).reshape(NSA_HEADS, tq, nk)


def _diff_kernel(t0, tq, ppt, has_past, lam_init, cases, *refs):
    tab_ref = refs[0]
    refs = refs[2 if has_past else 1:]
    q_ref, lam_ref, sub_ref, band_ref, kvn_ref = refs[:5]
    refs = refs[5:]
    if has_past:
        kvp_refs = refs[0:ppt]
        refs = refs[ppt:]
    o_ref, m_ref, acc_ref, qq_ref = refs
    p = pl.program_id(1)
    _flash_prologue(tab_ref, p, m_ref, acc_ref)

    @pl.when(tab_ref[T_FIRST, p] == 1)
    def _():
        first_map = lax.broadcasted_iota(jnp.int32, (tq, DIFF_DV), 1) < DIFF_DD
        for h in range(DIFF_HEADS):
            q_h = q_ref[0, :, h * DIFF_DV:(h + 1) * DIFF_DV].astype(F32)
            qq_ref[h] = jnp.concatenate([jnp.where(first_map, q_h, 0.0), jnp.where(first_map, 0.0, q_h)],
                                        axis=0).astype(qq_ref.dtype)

    fuse = DIFF_HEADS * 2 * tq <= FUSE_ROWS

    def step(is_new, bsel):
        tk = kvn_ref.shape[1] if is_new else ppt * PAGE
        s_list, pv_list = [], []
        if not is_new:
            by_head = [jnp.swapaxes(r[0].reshape(PAGE, 2 * DIFF_HEADS, DIFF_DV), 0, 1).astype(BF16)
                       for r in kvp_refs]
            page_rows = lambda row: jnp.concatenate([pg[row] for pg in by_head], axis=0)
        for h in range(DIFF_HEADS):
            if is_new:
                k_h = kvn_ref[0, :, h * DIFF_DV:(h + 1) * DIFF_DV]
                v_h = kvn_ref[0, :, D_MODEL + h * DIFF_DV:D_MODEL + (h + 1) * DIFF_DV]
            else:
                k_h, v_h = page_rows(h), page_rows(DIFF_HEADS + h)
            v_h = jnp.concatenate([v_h, _unit_block((tk, LANE), 1)], axis=1)
            s = _dot_t(qq_ref[h].astype(BF16), k_h)
            if bsel != BIAS_FAR:
                extra = _band_correction(band_ref, h, tq, tk, 0 if bsel == BIAS_DIAG else tk)
                if bsel == BIAS_DIAG:
                    extra = extra + _causal_neg(tq, tk)
                s = s + jnp.concatenate([extra, extra], axis=0)
            pv = functools.partial(lambda v, pb: jnp.dot(pb, v, preferred_element_type=F32), v_h)
            if fuse:
                s_list.append(s)
                pv_list.append(pv)
            else:
                _online_softmax_step(s, pv, m_ref.at[h], acc_ref.at[h])
        if fuse:
            _online_softmax_fused(s_list, pv_list, m_ref, acc_ref)

    _when_cases(tab_ref, p, cases, step)

    @pl.when(tab_ref[T_LAST, p] == 1)
    def _():
        lam = lam_ref[0:1, 0:1]
        for h in range(DIFF_HEADS):
            o = acc_ref[h, :, 0:DIFF_DV] / jnp.maximum(acc_ref[h, :, DIFF_DV:DIFF_DV + 1], TINY)
            o = o[0:tq] - lam * o[tq:2 * tq]
            y = o * lax.rsqrt(jnp.mean(o * o, axis=-1, keepdims=True) + EPS) * sub_ref[...]
            o_ref[0, :, h * DIFF_DV:(h + 1) * DIFF_DV] = y * (1.0 - lam_init)


def _diff_attn(q, lam, subln, band, new_rows, *, t0, tq, tkn, lam_init, past=None):
    nb, tt, _ = q.shape
    tkp = past[2] if past else tkn
    ppt = tkp // PAGE
    tab, cases = _pair_tables(tt // tq, tq, t0, tkp, tkn)
    qtile, const = _flash_specs(tq)
    new_spec = pl.BlockSpec((1, tkn, 2 * D_MODEL), lambda b, p, tab, *_: (b, tab[T_J, p] * tab[T_NEW, p], 0))
    in_specs = [qtile(D_MODEL), const(lam.shape), const(subln.shape), const(band.shape), new_spec]
    args = [q, lam, subln, band, new_rows]
    prefetch = [tab]
    if past:
        page_spec = lambda k: pl.BlockSpec(
            (1, PAGE * 2 * DIFF_HEADS, DIFF_DV), lambda b, p, tab, pt: (_past_page(tab, pt, b, p, ppt, k), 0, 0))
        in_specs += [page_spec(k) for k in range(ppt)]
        args += [past[1]] * ppt
        prefetch.append(past[0])
    return pl.pallas_call(
        functools.partial(_diff_kernel, t0, tq, ppt, past is not None, lam_init, cases),
        grid_spec=pltpu.PrefetchScalarGridSpec(
            num_scalar_prefetch=len(prefetch),
            grid=(nb, tab.shape[1]),
            in_specs=in_specs,
            out_specs=qtile(D_MODEL),
            scratch_shapes=[pltpu.VMEM((DIFF_HEADS, 2 * tq, LANE), F32),
                            pltpu.VMEM((DIFF_HEADS, 2 * tq, DIFF_DV + LANE), F32),
                            pltpu.VMEM((DIFF_HEADS, 2 * tq, DIFF_DV), BF16)]),
        out_shape=jax.ShapeDtypeStruct((nb, tt, D_MODEL), F32),
        compiler_params=_cparams(("parallel", "arbitrary")),
        name="diff_attn",
    )(*prefetch, *args)


def _out_mlp_kernel(x_ref, oc_ref, os_ref, ow_ref, od_ref, ga_ref, gb_ref, wo_ref, g_ref, wu_ref, wd_ref, y_ref):
    o_a = oc_ref[...] + os_ref[...] + ow_ref[...]
    o = ga_ref[...] * o_a + gb_ref[...] * od_ref[...]
    h = x_ref[...] + jnp.dot(o.astype(BF16), wo_ref[...], preferred_element_type=F32)
    hn = (h * lax.rsqrt(jnp.mean(h * h, axis=-1, keepdims=True) + EPS) * g_ref[...]).astype(BF16)
    y = h
    for c in range(D_FF // D_MODEL):
        cols = slice(c * D_MODEL, (c + 1) * D_MODEL)
        u = jnp.maximum(jnp.dot(hn, wu_ref[:, cols], preferred_element_type=F32), 0.0)
        y = y + jnp.dot((u * u).astype(BF16), wd_ref[cols, :], preferred_element_type=F32)
    y_ref[...] = y


def _out_mlp(x2d, oc, os_, ow, od, ga, gb, w_o, mlp_g, w_up, w_down, tm):
    m = x2d.shape[0]
    row = pl.BlockSpec((tm, D_MODEL), lambda i: (i, 0))
    const = lambda shape: pl.BlockSpec(shape, lambda i: (0, 0), pipeline_mode=pl.Buffered(1))
    return pl.pallas_call(
        _out_mlp_kernel,
        grid=(m // tm,),
        in_specs=[row] * 7 + [const(w_o.shape), const(mlp_g.shape), const(w_up.shape), const(w_down.shape)],
        out_specs=row,
        out_shape=jax.ShapeDtypeStruct((m, D_MODEL), F32),
        compiler_params=_cparams(("parallel",)),
        name="out_mlp",
    )(x2d, oc, os_, ow, od, ga, gb, w_o, mlp_g, w_up, w_down)


def _win_shift_kernel(n_new, past_ref, new_ref, o_ref):
    keep = o_ref.shape[2]
    o_ref[0] = jnp.concatenate([past_ref[0, :, n_new:keep], new_ref[0, :, 0:n_new]], axis=1)


def _win_shift(past_t, new_t, n_new):
    nb, feats, keep = past_t.shape
    return pl.pallas_call(
        functools.partial(_win_shift_kernel, n_new),
        grid=(nb,),
        in_specs=[pl.BlockSpec((1, feats, keep), lambda b: (b, 0, 0)),
                  pl.BlockSpec((1, feats, new_t.shape[2]), lambda b: (b, 0, 0))],
        out_specs=pl.BlockSpec((1, feats, keep), lambda b: (b, 0, 0)),
        out_shape=jax.ShapeDtypeStruct(past_t.shape, past_t.dtype),
        compiler_params=_cparams(("parallel",)),
        name="win_shift",
    )(past_t, new_t)


TQ_PROMPT = 256
TQ_FLASH = 512
TM_PROMPT = 256
TQ_SAMPLE = 8
TK_PAST = 1024


def _round_up(x, m):
    return -(-x // m) * m


def _feature_major(cache):
    lead = cache.shape[:-4]
    p, s, g, dk = cache.shape[-4:]
    n = len(lead)
    return jnp.transpose(cache, tuple(range(n)) + (n + 1, n + 2, n + 3, n)).reshape(lead + (s * g * dk, p))


def _position_major(feat, slots):
    lead = feat.shape[:-2]
    p = feat.shape[-1]
    n = len(lead)
    x = feat.reshape(lead + (slots, NSA_GROUPS, NSA_DK, p))
    return jnp.transpose(x, tuple(range(n)) + (n + 3, n, n + 1, n + 2))


def _layer_params(l, lam_init, rel_bias_table, attn_norm, w_in, nsa_q_gain, nsa_k_gain,
                  cmp_pe_k, cmp_w1_k, cmp_w2_k, cmp_pe_v, cmp_w1_v, cmp_w2_v,
                  diff_q_gain, diff_k_gain, lambda_q1, lambda_k1, lambda_q2, lambda_k2, diff_subln,
                  w_o, mlp_norm, w_up, w_down):
    tile4 = lambda g: jnp.tile(g.astype(F32), MXU_DIM // NSA_DK)
    gains = jnp.zeros((8, MXU_DIM), F32)
    for r, g in enumerate((nsa_q_gain[l], diff_q_gain[l], diff_k_gain[l])):
        gains = gains.at[r].set(tile4(g))
    flat_pe = lambda pe: jnp.broadcast_to(pe.reshape(1, CMP_BLOCK * NSA_DK), (8, CMP_BLOCK * NSA_DK))
    w1 = jnp.stack([cmp_w1_k[l], cmp_w1_v[l]])
    w2 = jnp.stack([cmp_w2_k[l], cmp_w2_v[l]])
    w2z = jnp.zeros((2, NSA_GROUPS, CMP_HID, GD), F32)
    for g in range(NSA_GROUPS):
        w2z = w2z.at[:, g, :, g * NSA_DK:(g + 1) * NSA_DK].set(w2)
    lam = (jnp.exp(jnp.sum(lambda_q1[l].astype(F32) * lambda_k1[l].astype(F32)))
           - jnp.exp(jnp.sum(lambda_q2[l].astype(F32) * lambda_k2[l].astype(F32))) + lam_init)
    w_row, w_kvt = _split_w_in(w_in[l])
    k_gain = tile4(nsa_k_gain[l])
    return dict(
        attn_g=attn_norm[l].reshape(1, D_MODEL), w_row=w_row, w_kvt=w_kvt, bd=_segment_mean_matrix(),
        gains=gains, k_gain=k_gain.reshape(1, GD), k_gain_col=k_gain.reshape(GD, 1),
        pe2=jnp.stack([flat_pe(cmp_pe_k[l]), flat_pe(cmp_pe_v[l])]), w1=w1, w1_bf=w1.astype(BF16),
        w2z=w2z.astype(BF16),
        tab_nsa_t=rel_bias_table[:, :NSA_HEADS].astype(F32).T, tab_diff_t=rel_bias_table[:, NSA_HEADS:].astype(F32).T,
        lam=jnp.broadcast_to(lam.reshape(1, 1), (8, LANE)), subln=diff_subln[l].reshape(1, DIFF_DV),
        w_o=w_o[l].astype(BF16), mlp_g=mlp_norm[l].reshape(1, D_MODEL),
        w_up=w_up[l].astype(BF16), w_down=w_down[l].astype(BF16))


def _run_proj(P, x, tm):
    kgain_t = jnp.broadcast_to(P['k_gain_col'], (GD, tm))
    return _proj(x, P['attn_g'], P['w_row'], P['w_kvt'], P['bd'], P['gains'], kgain_t, tm)


def _mixers(P, lam_init, q, g3, qd, sel_t, win_pieces, kvd, cmp_pages, tail, *,
            t0, tq, tqf, tkn, seq_len, nsa_past, diff_past):
    nb = q.shape[0]
    total = t0 + seq_len
    h0 = _pe_terms(P['pe2'], P['w1'])
    kc, vc = _compress(cmp_pages[0], cmp_pages[1], nb, cmp_pages[2], tail, P['w1_bf'], h0, P['w2z'], P['bd'],
                       P['k_gain'], page_table=cmp_pages[3])
    ncp = vc.shape[1]
    nc = -(-total // CMP_STRIDE) - CMP_BLOCK // CMP_STRIDE + 1
    n_slc = -(-total // SEL_BLOCK)
    nsp = _round_up(n_slc, LANE)
    bias_c, selwt = _cmp_tables(P['tab_nsa_t'], tqf, ncp, n_slc, nsp)
    o_cmp, msk = _cmp_topk(q, kc, vc, g3, bias_c, selwt, t0=t0, tq=tqf, nc=nc, n_slc=n_slc, n_top=min(N_SEL, n_slc))
    sel_past = dpast = None
    if nsa_past is not None:
        sel_past = (nsa_past[0], nsa_past[1], TK_PAST)
        dpast = (diff_past[0], diff_past[1], TK_PAST)
    o_sel = _sel_attn(q, msk, g3, _band_bias(P['tab_nsa_t']), sel_t, t0=t0, tq=tqf, tkn=tkn, past=sel_past)
    nk = sum(n for _, n, _ in win_pieces)
    o_win = _win_attn(q, g3, _window_bias(P['tab_nsa_t'], tq, nk), win_pieces, t0=t0, tq=tq)
    tkd = kvd.shape[1] if diff_past is not None else tkn
    o_diff = _diff_attn(qd, P['lam'], P['subln'], _band_bias(P['tab_diff_t']), kvd,
                        t0=t0, tq=tqf, tkn=tkd, lam_init=lam_init, past=dpast)
    return o_cmp, o_sel, o_win, o_diff


def _prompt_layer(P, lam_init, x):
    b, t, _ = x.shape
    tq = TQ_PROMPT
    q, nsa_t, win_t, g3, qd, diff5, ga, gb, kvd, sel_t = _run_proj(P, x, TM_PROMPT)
    back = WINDOW // tq
    pieces = [(win_t, tq, functools.partial(lambda k, i: jnp.maximum(i - k, 0), k)) for k in range(back, -1, -1)]
    tail = jnp.zeros((b, 2 * GD, PAGE), F32)
    cmp_pages = (nsa_t, lambda bb, page, pt: (bb, 0, page), t // PAGE, None)
    o_cmp, o_sel, o_win, o_diff = _mixers(
        P, lam_init, q, g3, qd, sel_t, pieces, kvd, cmp_pages, tail,
        t0=0, tq=tq, tqf=min(TQ_FLASH, t), tkn=min(TQ_FLASH, t), seq_len=t, nsa_past=None, diff_past=None)
    f2 = lambda a: a.reshape(b * t, D_MODEL)
    y = _out_mlp(f2(x), f2(o_cmp), f2(o_sel), f2(o_win), f2(o_diff), f2(ga), f2(gb),
                 P['w_o'], P['mlp_g'], P['w_up'], P['w_down'], TM_PROMPT)
    keep = min(WINDOW, t)
    return y.reshape(b, t, D_MODEL), _position_major(nsa_t, 4), diff5, _position_major(win_t[:, :, t - keep:], 2)


def _sample_layer(P, lam_init, x, cache_nsa, cache_diff, cache_win, page_table):
    nb, t, _ = x.shape
    tq = TQ_SAMPLE
    assert t <= tq and cache_win.shape[1] == WINDOW
    past_len = page_table.shape[1] * PAGE
    rows = nb * t
    q, nsa_t, win_t, g3, qd, diff5, ga, gb, kvd, sel_t = _run_proj(P, x.reshape(1, rows, D_MODEL), rows)
    per_seq = lambda a: jnp.pad(a.reshape(nb, t, a.shape[-1]), ((0, 0), (0, tq - t), (0, 0)))
    per_seq_t = lambda a: jnp.pad(jnp.transpose(a.reshape(a.shape[1], nb, t), (1, 0, 2)),
                                  ((0, 0), (0, 0), (0, PAGE - t)))
    nsa_new_t, win_new_t = per_seq_t(nsa_t), per_seq_t(win_t)
    pages_t = _feature_major(cache_nsa)
    past_win_t = _feature_major(cache_win)
    diff_pages = cache_diff.reshape(cache_diff.shape[0], PAGE * 2 * DIFF_HEADS, DIFF_DV)
    pieces = [(past_win_t, WINDOW, lambda i: 0), (win_new_t, PAGE, lambda i: 0)]
    cmp_pages = (pages_t, lambda bb, page, pt: (pt[bb, page], 0, 0), page_table.shape[1], page_table)
    o_cmp, o_sel, o_win, o_diff = _mixers(
        P, lam_init, per_seq(q[0]), per_seq(g3[0]), per_seq(qd[0]), per_seq_t(sel_t), pieces,
        jnp.pad(kvd.reshape(nb, t, 2 * D_MODEL), ((0, 0), (0, PAGE - t), (0, 0))), cmp_pages, nsa_new_t,
        t0=past_len, tq=tq, tqf=tq, tkn=PAGE, seq_len=t,
        nsa_past=(page_table, pages_t), diff_past=(page_table, diff_pages))
    f2 = lambda a: a[:, :t].reshape(rows, D_MODEL)
    y = _out_mlp(x.reshape(rows, D_MODEL), f2(o_cmp), f2(o_sel), f2(o_win), f2(o_diff), ga[0], gb[0],
                 P['w_o'], P['mlp_g'], P['w_up'], P['w_down'], rows)
    win_out_t = _win_shift(past_win_t, win_new_t, t)
    nsa_rows = _position_major(nsa_new_t[:, :, :t], 4)
    return (y.reshape(nb, t, D_MODEL), nsa_rows, diff5.reshape(nb, t, 2, DIFF_HEADS, DIFF_DV),
            _position_major(win_out_t, 2))


def kernel(x_prompt, x_sample, cache_nsa_kv, cache_diff_kv, cache_nsa_win, page_table, rel_bias_table, attn_norm, w_in, nsa_q_gain, nsa_k_gain, cmp_pe_k, cmp_w1_k, cmp_w2_k, cmp_pe_v, cmp_w1_v, cmp_w2_v, diff_q_gain, diff_k_gain, lambda_q1, lambda_k1, lambda_q2, lambda_k2, diff_subln, w_o, mlp_norm, w_up, w_down):
    depth = w_in.shape[0]
    yp, ys = x_prompt, x_sample
    outs = [[] for _ in range(6)]
    for l in range(depth):
        lam_init = 0.8 - 0.6 * math.exp(-0.3 * l)
        P = _layer_params(l, lam_init, rel_bias_table, attn_norm, w_in, nsa_q_gain, nsa_k_gain,
                          cmp_pe_k, cmp_w1_k, cmp_w2_k, cmp_pe_v, cmp_w1_v, cmp_w2_v,
                          diff_q_gain, diff_k_gain, lambda_q1, lambda_k1, lambda_q2, lambda_k2, diff_subln,
                          w_o, mlp_norm, w_up, w_down)
        yp, nsa_p, diff_p, win_p = _prompt_layer(P, lam_init, yp)
        ys, nsa_s, diff_s, win_s = _sample_layer(P, lam_init, ys, cache_nsa_kv[l], cache_diff_kv[l],
                                                 cache_nsa_win[l], page_table)
        for dst, v in zip(outs, (nsa_p, diff_p, win_p, nsa_s, diff_s, win_s)):
            dst.append(v)
    return (yp, ys) + tuple(jnp.stack(o, 0) for o in outs)
```

```python
import functools
import math

import numpy as np
import jax
import jax.numpy as jnp
from jax import lax
from jax.experimental import pallas as pl
from jax.experimental.pallas import tpu as pltpu

F32 = jnp.float32
BF16 = jnp.bfloat16

D_MODEL = 1024
NSA_HEADS = 16
NSA_GROUPS = 4
NSA_HPG = NSA_HEADS // NSA_GROUPS
NSA_DK = D_MODEL // NSA_HEADS
GD = NSA_GROUPS * NSA_DK
CMP_BLOCK = 32
CMP_STRIDE = 16
CMP_HID = 4 * NSA_DK
SEL_BLOCK = 64
N_SEL = 16
WINDOW = 512
FORCE_BONUS = 1e4
DIFF_HEADS = 8
DIFF_DD = D_MODEL // (2 * DIFF_HEADS)
DIFF_DV = 2 * DIFF_DD
D_FF = 4 * D_MODEL
NUM_BUCKETS = 32
MAX_DISTANCE = 128
EPS = 1e-6
NEG_INF = -1e30
TINY = 1e-30
QK_SCALE = NSA_DK ** -0.5

LANE = 128
MXU_DIM = 256
VMEM_LIMIT = 56 * 1024 * 1024
PAGE = 128

C_QN = 0
C_QD = C_QN + D_MODEL
C_KD = C_QD + D_MODEL
C_VD = C_KD + D_MODEL
C_GA = C_VD + D_MODEL
C_GB = C_GA + D_MODEL
C_G3 = C_GB + D_MODEL
N_ROWMAJOR = C_G3 + LANE


def _cparams(sem):
    return pltpu.CompilerParams(dimension_semantics=sem, vmem_limit_bytes=VMEM_LIMIT)


def _t5_bucket_np(dist):
    n = np.maximum(dist, 0)
    max_exact = NUM_BUCKETS // 2
    nf = np.maximum(n, 1).astype(np.float32)
    log_ratio = np.log(nf / np.float32(max_exact)) / np.float32(math.log(MAX_DISTANCE / max_exact))
    large = max_exact + (log_ratio * np.float32(NUM_BUCKETS - max_exact)).astype(np.int32)
    large = np.minimum(large, NUM_BUCKETS - 1)
    return np.where(n < max_exact, n, large).astype(np.int32)


def _segment_mean_matrix():
    i = np.arange(MXU_DIM)
    return jnp.asarray((i[:, None] // NSA_DK == i[None, :] // NSA_DK) / NSA_DK, dtype=BF16)


def _dot_t(a, b, precision=None):
    return lax.dot_general(a, b, (((1,), (1,)), ((), ())), preferred_element_type=F32, precision=precision)


def _bias_lookup_kernel(hp, tab_ref, idx_ref, o_ref):
    n, r, c = idx_ref.shape
    for k in range(n):
        idx = idx_ref[k]
        for hh in range(hp):
            h = pl.program_id(0) * hp + hh
            acc = jnp.zeros((r, c), F32)
            for b in range(NUM_BUCKETS):
                acc = jnp.where(idx == b, tab_ref[h, b], acc)
            o_ref[0, k, hh * r:(hh + 1) * r, :] = acc


def _bias_lookup(tab_t, idx, hp=1):
    heads = tab_t.shape[0]
    n, r, c = idx.shape
    return pl.pallas_call(
        functools.partial(_bias_lookup_kernel, hp),
        grid=(heads // hp,),
        in_specs=[pl.BlockSpec(memory_space=pltpu.SMEM),
                  pl.BlockSpec((n, r, c), lambda g: (0, 0, 0), pipeline_mode=pl.Buffered(1))],
        out_specs=pl.BlockSpec((1, n, hp * r, c), lambda g: (g, 0, 0, 0)),
        out_shape=jax.ShapeDtypeStruct((heads // hp, n, hp * r, c), F32),
        compiler_params=_cparams(("parallel",)),
        name="bias_lookup",
    )(tab_t, jnp.asarray(idx, dtype=jnp.int32))


def _proj_kernel(x_ref, g_ref, w_ref, wt_ref, bd_ref, gains_ref, kgt_ref,
                 q_ref, nsat_ref, wint_ref, g3_ref, qd_ref, diff_ref, ga_ref, gb_ref, kvd_ref, selt_ref):
    x = x_ref[0]
    xn = x * lax.rsqrt(jnp.mean(x * x, axis=-1, keepdims=True) + EPS) * g_ref[...]
    xn = xn.astype(BF16)
    bd = bd_ref[...]

    def mm(c0, width=MXU_DIM):
        return jnp.dot(xn, w_ref[:, c0:c0 + width], preferred_element_type=F32)

    def headnorm(acc, gain_row):
        ms = jnp.dot((acc * acc).astype(BF16), bd, preferred_element_type=F32)
        return acc * lax.rsqrt(ms + EPS) * gains_ref[gain_row:gain_row + 1, :]

    heads_per_tile = MXU_DIM // DIFF_DV
    for c in range(D_MODEL // MXU_DIM):
        o = c * MXU_DIM
        q_ref[0, :, o:o + MXU_DIM] = (headnorm(mm(C_QN + o), 0) * QK_SCALE).astype(BF16)
        qd_ref[0, :, o:o + MXU_DIM] = (headnorm(mm(C_QD + o), 1) * QK_SCALE).astype(BF16)
        kd = headnorm(mm(C_KD + o), 2)
        vd = mm(C_VD + o)
        for hh in range(heads_per_tile):
            h = c * heads_per_tile + hh
            diff_ref[0, :, 0, h, :] = kd[:, hh * DIFF_DV:(hh + 1) * DIFF_DV]
            diff_ref[0, :, 1, h, :] = vd[:, hh * DIFF_DV:(hh + 1) * DIFF_DV]
        kvd_ref[0, :, o:o + MXU_DIM] = kd.astype(BF16)
        kvd_ref[0, :, D_MODEL + o:D_MODEL + o + MXU_DIM] = vd.astype(BF16)
        ga_ref[0, :, o:o + MXU_DIM] = jax.nn.sigmoid(mm(C_GA + o))
        gb_ref[0, :, o:o + MXU_DIM] = jax.nn.sigmoid(mm(C_GB + o))
    g3_ref[0] = jax.nn.sigmoid(mm(C_G3, LANE))

    def mm_t(slot):
        return _dot_t(wt_ref[slot * GD:(slot + 1) * GD, :], xn)

    def headnorm_t(acc):
        ms = jnp.dot(bd, (acc * acc).astype(BF16), preferred_element_type=F32)
        return acc * lax.rsqrt(ms + EPS) * kgt_ref[...]

    nsat_ref[0, 0:GD, :] = mm_t(0)
    nsat_ref[0, GD:2 * GD, :] = mm_t(1)
    k_sel, v_sel = headnorm_t(mm_t(2)), mm_t(3)
    nsat_ref[0, 2 * GD:3 * GD, :] = k_sel
    nsat_ref[0, 3 * GD:4 * GD, :] = v_sel
    selt_ref[0, 0:GD, :] = k_sel.astype(BF16)
    selt_ref[0, GD:2 * GD, :] = v_sel.astype(BF16)
    wint_ref[0, 0:GD, :] = headnorm_t(mm_t(4))
    wint_ref[0, GD:2 * GD, :] = mm_t(5)


def _proj(x, attn_g, w_row, w_kvt, bd, gains, kgain_t, tm):
    nb, t, _ = x.shape
    row = lambda width: pl.BlockSpec((1, tm, width), lambda b, i: (b, i, 0))
    col = lambda feats: pl.BlockSpec((1, feats, tm), lambda b, i: (b, 0, i))
    const = lambda shape: pl.BlockSpec(shape, lambda b, i: (0, 0), pipeline_mode=pl.Buffered(1))
    out_specs = (row(D_MODEL), col(4 * GD), col(2 * GD), row(LANE), row(D_MODEL),
                 pl.BlockSpec((1, tm, 2, DIFF_HEADS, DIFF_DV), lambda b, i: (b, i, 0, 0, 0)),
                 row(D_MODEL), row(D_MODEL), row(2 * D_MODEL), col(2 * GD))
    out_shapes = (
        jax.ShapeDtypeStruct((nb, t, D_MODEL), BF16),
        jax.ShapeDtypeStruct((nb, 4 * GD, t), F32),
        jax.ShapeDtypeStruct((nb, 2 * GD, t), F32),
        jax.ShapeDtypeStruct((nb, t, LANE), F32),
        jax.ShapeDtypeStruct((nb, t, D_MODEL), BF16),
        jax.ShapeDtypeStruct((nb, t, 2, DIFF_HEADS, DIFF_DV), F32),
        jax.ShapeDtypeStruct((nb, t, D_MODEL), F32),
        jax.ShapeDtypeStruct((nb, t, D_MODEL), F32),
        jax.ShapeDtypeStruct((nb, t, 2 * D_MODEL), BF16),
        jax.ShapeDtypeStruct((nb, 2 * GD, t), BF16),
    )
    return pl.pallas_call(
        _proj_kernel,
        grid=(nb, t // tm),
        in_specs=[row(D_MODEL), const((1, D_MODEL)), const(w_row.shape), const(w_kvt.shape),
                  const(bd.shape), const(gains.shape), const(kgain_t.shape)],
        out_specs=out_specs,
        out_shape=out_shapes,
        compiler_params=_cparams(("parallel", "parallel")),
        name="proj",
    )(x, attn_g, w_row, w_kvt, bd, gains, kgain_t)


def _split_w_in(w_in):
    sizes = (D_MODEL, 6 * GD, 3 * NSA_HEADS, D_MODEL, D_MODEL, D_MODEL, D_MODEL, D_MODEL)
    pts = np.cumsum(sizes)[:-1]
    q_n, kv_n, g_n, q_d, k_d, v_d, ga, gb = jnp.split(w_in, pts, axis=-1)
    g_pad = jnp.pad(g_n, ((0, 0), (0, LANE - 3 * NSA_HEADS)))
    w_row = jnp.concatenate([q_n, q_d, k_d, v_d, ga, gb, g_pad], axis=-1).astype(BF16)
    return w_row, kv_n.T.astype(BF16)


CH_PER_PAGE = PAGE // CMP_STRIDE
J_PER_TILE = MXU_DIM // NSA_DK


def _pe_kernel(pe_ref, w1_ref, o_ref):
    for kv in range(2):
        o_ref[kv] = jnp.dot(pe_ref[kv], w1_ref[kv], preferred_element_type=F32,
                            precision=lax.Precision.HIGHEST)


def _pe_terms(pe2, w1):
    return pl.pallas_call(
        _pe_kernel,
        out_shape=jax.ShapeDtypeStruct((2, 8, CMP_HID), F32),
        name="cmp_pe",
    )(pe2, w1)


def _compress_kernel(pps, *refs):
    page_refs = refs[1:1 + pps]
    next_ref, tail_ref, w1_ref, h0_ref, w2_ref, bd_ref, kg_ref, kc_ref, vc_ref = refs[1 + pps:10 + pps]
    rows_refs = refs[10 + pps:]
    is_last = pl.program_id(1) == pl.num_programs(1) - 1
    cb = pps * CH_PER_PAGE
    rows = cb + CH_PER_PAGE
    gpl = LANE // NSA_DK
    for fb, rows_ref in enumerate(rows_refs):
        feats = slice(fb * LANE, (fb + 1) * LANE)
        for k, p in enumerate(page_refs):
            rows_ref[k * PAGE:(k + 1) * PAGE, :] = p[0, feats, :].T
        rows_ref[pps * PAGE:(pps + 1) * PAGE, :] = jnp.where(is_last, tail_ref[0, feats, :], next_ref[0, feats, :]).T

    for kv in range(2):
        acc = [None, None]
        for jt in range(CMP_STRIDE // J_PER_TILE):
            xs = [[rows_refs[kv * (GD // LANE) + half][pl.ds(jt * J_PER_TILE + jj, rows, stride=CMP_STRIDE), :]
                   for half in range(GD // LANE)] for jj in range(J_PER_TILE)]
            lhs = jnp.concatenate(
                [jnp.concatenate([x[g // gpl][:, (g % gpl) * NSA_DK:(g % gpl + 1) * NSA_DK] for x in xs], axis=1)
                 for g in range(NSA_GROUPS)], axis=0).astype(BF16)
            for r in range(2):
                k0 = r * CMP_STRIDE * NSA_DK + jt * MXU_DIM
                part = jnp.dot(lhs, w1_ref[kv, k0:k0 + MXU_DIM, :], preferred_element_type=F32)
                acc[r] = part if acc[r] is None else acc[r] + part
        out = None
        for g in range(NSA_GROUPS):
            h = h0_ref[kv, 0:1, :] + acc[0][g * rows:g * rows + cb] + acc[1][g * rows + 1:g * rows + 1 + cb]
            part = jnp.dot(jax.nn.gelu(h).astype(BF16), w2_ref[kv, g], preferred_element_type=F32)
            out = part if out is None else out + part
        if kv == 0:
            ms = jnp.dot((out * out).astype(BF16), bd_ref[...], preferred_element_type=F32)
            kc_ref[0] = (out * lax.rsqrt(ms + EPS) * kg_ref[...]).T
        else:
            vc_ref[0] = out


def _compress(pages, page_index, nb, npg, tail, w1, h0, w2z, bd, kgain, page_table=None):
    pps = min(16, npg)
    assert npg % pps == 0
    prefetch = [page_table if page_table is not None else jnp.zeros((1, 1), jnp.int32)]
    blk = (1, 2 * GD, PAGE)
    page_spec = lambda k: pl.BlockSpec(blk, lambda b, s, pt: page_index(b, s * pps + k, pt))
    next_spec = pl.BlockSpec(blk, lambda b, s, pt: page_index(b, jnp.minimum(s * pps + pps, npg - 1), pt))
    const = lambda shape: pl.BlockSpec(shape, lambda b, s, pt: (0,) * len(shape), pipeline_mode=pl.Buffered(1))
    out_spec = pl.BlockSpec((1, pps * CH_PER_PAGE, GD), lambda b, s, pt: (b, s, 0))
    out_sds = jax.ShapeDtypeStruct((nb, npg * CH_PER_PAGE, GD), F32)
    return pl.pallas_call(
        functools.partial(_compress_kernel, pps),
        grid_spec=pltpu.PrefetchScalarGridSpec(
            num_scalar_prefetch=1,
            grid=(nb, npg // pps),
            in_specs=[page_spec(k) for k in range(pps)] + [
                next_spec,
                pl.BlockSpec(blk, lambda b, s, pt: (b, 0, 0)),
                const(w1.shape), const(h0.shape), const(w2z.shape), const(bd.shape), const(kgain.shape)],
            out_specs=(pl.BlockSpec((1, GD, pps * CH_PER_PAGE), lambda b, s, pt: (b, 0, s)), out_spec),
            scratch_shapes=[pltpu.VMEM(((pps + 1) * PAGE, LANE), F32)] * (2 * GD // LANE)),
        out_shape=(jax.ShapeDtypeStruct((nb, GD, npg * CH_PER_PAGE), F32), out_sds),
        compiler_params=_cparams(("parallel", "arbitrary")),
        name="compress",
    )(*prefetch, *([pages] * (pps + 1)), tail, w1, h0, w2z, bd, kgain)


CMP_WIN = 64
CMP_WIN_BACK = 32
SCORE_PAD = -(2.0 ** 126)


def _gate_col(g3, head, branch):
    c = head * 3 + branch
    return g3[:, c:c + 1]


def _cmp_topk_kernel(t0, tq, ncp, nc, n_slc, nsp, n_top,
                     q_ref, kct_ref, vc_ref, g3_ref, bias_ref, selwt_ref, o_ref, msk_ref):
    qpos0 = t0 + pl.program_id(1) * tq
    col = lax.broadcasted_iota(jnp.int32, (tq, ncp), 1)
    qpos = qpos0 + lax.broadcasted_iota(jnp.int32, (tq, ncp), 0)
    hidden = jnp.where((col * CMP_STRIDE + (CMP_BLOCK - 1) <= qpos) & (col < nc), 0.0, MASK_NEG)
    any_visible = qpos0 + lax.broadcasted_iota(jnp.int32, (tq, 1), 0) >= CMP_BLOCK - 1
    c0 = qpos0 // CMP_STRIDE - CMP_WIN_BACK
    place = (lax.broadcasted_iota(jnp.int32, (CMP_WIN, ncp), 1)
             == lax.broadcasted_iota(jnp.int32, (CMP_WIN, ncp), 0) + c0).astype(BF16)
    g3 = g3_ref[0]
    n_lanes = _round_up(NSA_GROUPS * tq, LANE)
    blk = lax.broadcasted_iota(jnp.int32, (nsp, n_lanes), 0)
    bpos = qpos0 + lax.broadcasted_iota(jnp.int32, (nsp, n_lanes), 1) % tq
    cur = bpos // SEL_BLOCK
    forced = (blk == 0) | (blk == cur) | (blk == cur - 1)
    valid_b = blk * SEL_BLOCK <= bpos
    outs, psums = [], []
    for g in range(NSA_GROUPS):
        kc_g = jnp.concatenate([place, place, kct_ref[0, g * NSA_DK:(g + 1) * NSA_DK, :].astype(BF16)], axis=0)
        vc_g = vc_ref[0, :, g * NSA_DK:(g + 1) * NSA_DK].astype(BF16)
        psum = jnp.zeros((tq, ncp), F32)
        for hl in range(NSA_HPG):
            h = g * NSA_HPG + hl
            q_h = jnp.concatenate([bias_ref[h], q_ref[0, :, h * NSA_DK:(h + 1) * NSA_DK]], axis=1)
            s = jnp.dot(q_h, kc_g, preferred_element_type=F32) + hidden
            p = jnp.exp(s - jnp.max(s, axis=1)[:, None])
            inv = jnp.where(any_visible, 1.0 / jnp.maximum(jnp.sum(p, axis=1)[:, None], TINY), 0.0)
            p = p * inv
            o_h = jnp.dot(p.astype(BF16), vc_g, preferred_element_type=F32)
            outs.append(o_h * _gate_col(g3, h, 0))
            psum = psum + p
        psums.append(psum)
    if n_lanes != NSA_GROUPS * tq:
        psums.append(jnp.zeros((n_lanes - NSA_GROUPS * tq, ncp), F32))
    imp_t = _dot_t(selwt_ref[...], jnp.concatenate(psums, axis=0), precision=lax.Precision.HIGHEST)
    score = jnp.where(valid_b, imp_t + jnp.where(forced, FORCE_BONUS, 0.0), -FORCE_BONUS)
    score = jnp.where(blk < n_slc, score, SCORE_PAD)

    def pick(_, sc):
        best = jnp.max(sc, axis=0, keepdims=True)
        first = jnp.min(jnp.where(sc == best, blk, nsp), axis=0, keepdims=True)
        return jnp.where(blk == first, SCORE_PAD, sc)

    picked = lax.fori_loop(0, n_top, pick, score)
    hidden_blocks = jnp.where((picked < 0.5 * SCORE_PAD) & (blk < n_slc), 0.0, MASK_NEG).T
    for g in range(NSA_GROUPS):
        msk_ref[0, :, g * nsp:(g + 1) * nsp] = hidden_blocks[g * tq:(g + 1) * tq].astype(BF16)
    o_ref[0] = jnp.concatenate(outs, axis=1)


def _cmp_topk(q, kct, vc, g3, bias, selwt, *, t0, tq, nc, n_slc, n_top):
    nb, tt, _ = q.shape
    ncp = vc.shape[1]
    nsp = selwt.shape[0]
    const = lambda shape: pl.BlockSpec(shape, lambda b, i: (0,) * len(shape), pipeline_mode=pl.Buffered(1))
    per_b = lambda shape: pl.BlockSpec(shape, lambda b, i: (b, 0, 0))
    tile = lambda width: pl.BlockSpec((1, tq, width), lambda b, i: (b, i, 0))
    return pl.pallas_call(
        functools.partial(_cmp_topk_kernel, t0, tq, ncp, nc, n_slc, nsp, n_top),
        grid=(nb, tt // tq),
        in_specs=[tile(D_MODEL), per_b((1, GD, ncp)), per_b((1, ncp, GD)), tile(LANE),
                  const(bias.shape), const(selwt.shape)],
        out_specs=(tile(D_MODEL), tile(NSA_GROUPS * nsp)),
        out_shape=(jax.ShapeDtypeStruct((nb, tt, D_MODEL), F32),
                   jax.ShapeDtypeStruct((nb, tt, NSA_GROUPS * nsp), BF16)),
        compiler_params=_cparams(("parallel", "arbitrary")),
        name="cmp_topk",
    )(q, kct, vc, g3, bias, selwt)


def _cmp_tables(tab_nsa_t, tq, ncp, n_slc, nsp):
    r = np.arange(tq)[:, None]
    w = np.arange(CMP_WIN)[None, :]
    dc = r - CMP_STRIDE * (w - CMP_WIN_BACK) - (CMP_BLOCK - 1)
    idx = np.where(dc >= 0, _t5_bucket_np(dc), NUM_BUCKETS - 1)
    win = _bias_lookup(tab_nsa_t, idx[None])[:, 0] - tab_nsa_t[:, NUM_BUCKETS - 1].reshape(-1, 1, 1)
    high = win.astype(BF16)
    bias = jnp.concatenate([high, (win - high.astype(F32)).astype(BF16)], axis=-1)
    c_lo = np.arange(ncp)[None, :] * CMP_STRIDE
    s_lo = np.arange(nsp)[:, None] * SEL_BLOCK
    shared = np.clip(np.minimum(c_lo + CMP_BLOCK, s_lo + SEL_BLOCK) - np.maximum(c_lo, s_lo), 0, None)
    shared = np.where(np.arange(nsp)[:, None] < n_slc, shared, 0)
    return bias, jnp.asarray(shared / CMP_BLOCK, dtype=F32)


BIAS_DIAG, BIAS_PREV, BIAS_FAR = 0, 1, 2
T_I, T_J, T_FIRST, T_LAST, T_BIAS, T_NEW, T_KPOS = range(7)


def _pair_tables(nq, tq, t0, tkp, tkn):
    assert t0 % tkp == 0
    n_past = t0 // tkp
    rows = []
    for i in range(nq):
        q0 = t0 + i * tq
        tiles = [(0, j, j * tkp, tkp) for j in range(n_past)]
        tiles += [(1, j, t0 + j * tkn, tkn) for j in range((i * tq + tq - 1) // tkn + 1)]
        for n, (is_new, j, k0, tk) in enumerate(tiles):
            delta = q0 - k0
            assert delta in (0, tk) or delta >= tk + MAX_DISTANCE - 1, (tq, t0, tkp, tkn)
            bsel = BIAS_DIAG if delta == 0 else BIAS_PREV if delta == tk else BIAS_FAR
            rows.append((i, j, int(n == 0), int(n == len(tiles) - 1), bsel, is_new, k0))
    tab = np.array(rows, dtype=np.int32).T.copy()
    cases = sorted({(int(r[T_NEW]), int(r[T_BIAS])) for r in tab.T})
    return jnp.asarray(tab), tuple(cases)


BAND = MAX_DISTANCE


def _band_bias(tab_t):
    r = np.arange(BAND)[:, None]
    c = np.arange(BAND)[None, :]
    tiles = _bias_lookup(tab_t, np.stack([_t5_bucket_np(r - c), _t5_bucket_np(r - c + BAND)]))
    return tiles - tab_t[:, NUM_BUCKETS - 1].reshape(-1, 1, 1, 1)


def _when_cases(tab_ref, p, cases, fn):
    for is_new, bsel in cases:
        pl.when((tab_ref[T_NEW, p] == is_new) & (tab_ref[T_BIAS, p] == bsel))(functools.partial(fn, is_new, bsel))


MASK_NEG = -1e30


def _tile_dtype(rows):
    return BF16 if rows % 16 == 0 else F32


def _band_correction(band_ref, head, tq, tk, delta):
    rb_rows, cb_cols = min(tq, BAND), min(tk, BAND)
    rows = []
    for rb in range(max(tq // BAND, 1)):
        cols = []
        for cb in range(max(tk // BAND, 1)):
            d = delta // BAND + rb - cb
            cols.append(band_ref[head, d, 0:rb_rows, 0:cb_cols] if d in (0, 1) else jnp.zeros((rb_rows, cb_cols), F32))
        rows.append(cols[0] if len(cols) == 1 else jnp.concatenate(cols, axis=1))
    return rows[0] if len(rows) == 1 else jnp.concatenate(rows, axis=0)


def _lane_tiles(x, n):
    return x if n == 1 else jnp.concatenate([x] * n, axis=1)


def _online_softmax_step(s, pv, m_ref, acc_ref):
    m_prev = m_ref[...]
    m_next = jnp.maximum(m_prev, jnp.max(s, axis=1)[:, None])
    p = jnp.exp(s - _lane_tiles(m_next, s.shape[1] // LANE))
    alpha = jnp.exp(m_prev - m_next)
    m_ref[...] = m_next
    acc_ref[...] = _lane_tiles(alpha, acc_ref.shape[1] // LANE) * acc_ref[...] + pv(p.astype(BF16))


def _online_softmax_fused(s_list, pv_list, m_ref, acc_ref):
    n, rows = len(s_list), s_list[0].shape[0]
    width = acc_ref.shape[2]
    s = jnp.concatenate(s_list, axis=0)
    m_prev = m_ref[...].reshape(n * rows, LANE)
    m_next = jnp.maximum(m_prev, jnp.max(s, axis=1)[:, None])
    p = jnp.exp(s - _lane_tiles(m_next, s.shape[1] // LANE)).astype(BF16)
    alpha = jnp.exp(m_prev - m_next)
    m_ref[...] = m_next.reshape(n, rows, LANE)
    pv = jnp.concatenate([pv_list[k](p[k * rows:(k + 1) * rows]) for k in range(n)], axis=0)
    acc = _lane_tiles(alpha, width // LANE) * acc_ref[...].reshape(n * rows, width) + pv
    acc_ref[...] = acc.reshape(n, rows, width)


FUSE_ROWS = 128


def _unit_block(shape, axis):
    return jnp.where(lax.broadcasted_iota(jnp.int32, shape, axis) == 0, 1.0, 0.0).astype(BF16)


def _causal_neg(tq, tk):
    r = lax.broadcasted_iota(jnp.int32, (tq, tk), 0)
    c = lax.broadcasted_iota(jnp.int32, (tq, tk), 1)
    return jnp.where(c <= r, 0.0, MASK_NEG).astype(F32)


def _flash_prologue(tab_ref, p, m_ref, acc_ref):
    @pl.when(tab_ref[T_FIRST, p] == 1)
    def _():
        m_ref[...] = jnp.full(m_ref.shape, NEG_INF, F32)
        acc_ref[...] = jnp.zeros(acc_ref.shape, F32)


def _flash_specs(tq):
    qtile = lambda width: pl.BlockSpec((1, tq, width), lambda b, p, tab, *_: (b, tab[T_I, p], 0))
    const = lambda shape: pl.BlockSpec(shape, lambda b, p, tab, *_: (0,) * len(shape), pipeline_mode=pl.Buffered(1))
    return qtile, const


def _past_page(tab, pt, b, p, ppt, k):
    n_past = pt.shape[1] // ppt
    j = jnp.where(tab[T_NEW, p] == 1, n_past - 1, tab[T_J, p])
    return pt[b, j * ppt + k]


def _sel_kernel(t0, tq, ppt, has_past, nsp, cases, *refs):
    tab_ref = refs[0]
    refs = refs[2 if has_past else 1:]
    q_ref, msk_ref, g3_ref, band_ref, kvn_ref = refs[:5]
    refs = refs[5:]
    if has_past:
        kp_refs, vp_refs = refs[0:ppt], refs[ppt:2 * ppt]
        refs = refs[2 * ppt:]
    o_ref, m_ref, acc_ref, qh_ref = refs
    p = pl.program_id(1)
    kpos0 = tab_ref[T_KPOS, p]
    _flash_prologue(tab_ref, p, m_ref, acc_ref)
    stack = qh_ref.shape[1] // tq
    fuse = NSA_HEADS * tq <= FUSE_ROWS

    @pl.when(tab_ref[T_FIRST, p] == 1)
    def _():
        for hs in range(NSA_HEADS // stack):
            g = hs * stack // NSA_HPG
            q_hs = jnp.concatenate(
                [q_ref[0, :, h * NSA_DK:(h + 1) * NSA_DK] for h in range(hs * stack, (hs + 1) * stack)], axis=0)
            msk_g = msk_ref[0, :, g * nsp:(g + 1) * nsp]
            msk_hs = msk_g if stack == 1 else jnp.concatenate([msk_g] * stack, axis=0)
            qh_ref[hs] = jnp.concatenate([q_hs, msk_hs], axis=1).astype(qh_ref.dtype)

    def step(is_new, bsel):
        if is_new:
            kt_all, vt_all = kvn_ref[0, 0:GD, :], kvn_ref[0, GD:2 * GD, :]
        else:
            kt_all = jnp.concatenate([r[0] for r in kp_refs], axis=1)
            vt_all = jnp.concatenate([r[0] for r in vp_refs], axis=1)
        tk = kt_all.shape[1]
        blk_of_key = (kpos0 + lax.broadcasted_iota(jnp.int32, (nsp, tk), 1)) // SEL_BLOCK
        expand = (lax.broadcasted_iota(jnp.int32, (nsp, tk), 0) == blk_of_key).astype(BF16)
        s_list, pv_list = [], []
        causal = _causal_neg(tq, tk) if bsel == BIAS_DIAG else None
        for g in range(NSA_GROUPS):
            kt_g = jnp.concatenate([kt_all[g * NSA_DK:(g + 1) * NSA_DK, :].astype(BF16), expand], axis=0)
            vt_g = jnp.concatenate([vt_all[g * NSA_DK:(g + 1) * NSA_DK, :].astype(BF16),
                                    _unit_block((LANE - NSA_DK, tk), 0)], axis=0)
            for hs in range(g * NSA_HPG // stack, (g + 1) * NSA_HPG // stack):
                s = jnp.dot(qh_ref[hs].astype(BF16), kt_g, preferred_element_type=F32)
                if bsel != BIAS_FAR:
                    extra = [_band_correction(band_ref, h, tq, tk, 0 if bsel == BIAS_DIAG else tk)
                             for h in range(hs * stack, (hs + 1) * stack)]
                    if causal is not None:
                        extra = [e + causal for e in extra]
                    s = s + (extra[0] if stack == 1 else jnp.concatenate(extra, axis=0))
                pv = functools.partial(lambda vt, pb: _dot_t(pb, vt), vt_g)
                if fuse:
                    s_list.append(s)
                    pv_list.append(pv)
                else:
                    _online_softmax_step(s, pv, m_ref.at[hs], acc_ref.at[hs])
        if fuse:
            _online_softmax_fused(s_list, pv_list, m_ref, acc_ref)

    _when_cases(tab_ref, p, cases, step)

    @pl.when(tab_ref[T_LAST, p] == 1)
    def _():
        g3 = g3_ref[0]
        outs = []
        for h in range(NSA_HEADS):
            rows = slice((h % stack) * tq, (h % stack + 1) * tq)
            o = acc_ref[h // stack, rows, 0:NSA_DK] / jnp.maximum(acc_ref[h // stack, rows, NSA_DK:NSA_DK + 1], TINY)
            outs.append(o * _gate_col(g3, h, 1))
        o_ref[0] = jnp.concatenate(outs, axis=1)


def _sel_attn(q, msk, g3, band, new_t, *, t0, tq, tkn, past=None):
    nb, tt, _ = q.shape
    nsp = msk.shape[2] // NSA_GROUPS
    tkp = past[2] if past else tkn
    ppt = tkp // PAGE
    tab, cases = _pair_tables(tt // tq, tq, t0, tkp, tkn)
    stack = NSA_HPG if NSA_HPG * tq <= LANE else 1
    rows = stack * tq
    qtile, const = _flash_specs(tq)
    new_spec = pl.BlockSpec((1, 2 * GD, tkn), lambda b, p, tab, *_: (b, 0, tab[T_J, p] * tab[T_NEW, p]))
    in_specs = [qtile(D_MODEL), qtile(NSA_GROUPS * nsp), qtile(LANE), const(band.shape), new_spec]
    args = [q, msk, g3, band, new_t]
    prefetch = [tab]
    if past:
        page_spec = lambda slot, k: pl.BlockSpec(
            (1, GD, PAGE), lambda b, p, tab, pt: (_past_page(tab, pt, b, p, ppt, k), slot, 0))
        in_specs += [page_spec(s, k) for s in (2, 3) for k in range(ppt)]
        args += [past[1]] * (2 * ppt)
        prefetch.append(past[0])
    return pl.pallas_call(
        functools.partial(_sel_kernel, t0, tq, ppt, past is not None, nsp, cases),
        grid_spec=pltpu.PrefetchScalarGridSpec(
            num_scalar_prefetch=len(prefetch),
            grid=(nb, tab.shape[1]),
            in_specs=in_specs,
            out_specs=qtile(D_MODEL),
            scratch_shapes=[pltpu.VMEM((NSA_HEADS // stack, rows, LANE), F32),
                            pltpu.VMEM((NSA_HEADS // stack, rows, LANE), F32),
                            pltpu.VMEM((NSA_HEADS // stack, rows, NSA_DK + nsp), _tile_dtype(rows))]),
        out_shape=jax.ShapeDtypeStruct((nb, tt, D_MODEL), F32),
        compiler_params=_cparams(("parallel", "arbitrary")),
        name="sel_attn",
    )(*prefetch, *args)


def _win_kernel(t0, tq, n_piece, *refs):
    q_ref, g3_ref, bias_ref = refs[:3]
    piece_refs = refs[3:3 + n_piece]
    o_ref = refs[3 + n_piece]
    cols = [r[0] for r in piece_refs]
    kvt = cols[0] if n_piece == 1 else jnp.concatenate(cols, axis=1)
    nk = kvt.shape[1]
    qpos0 = t0 + pl.program_id(1) * tq
    r = lax.broadcasted_iota(jnp.int32, (tq, nk), 0)
    c = lax.broadcasted_iota(jnp.int32, (tq, nk), 1)
    dw = r - c + WINDOW
    hidden = jnp.where((qpos0 - WINDOW + c >= 0) & (dw >= 0) & (dw < WINDOW), 0.0, MASK_NEG)
    g3 = g3_ref[0]
    outs = []
    for g in range(NSA_GROUPS):
        kt_g = kvt[g * NSA_DK:(g + 1) * NSA_DK, :].astype(BF16)
        vt_g = jnp.concatenate([kvt[GD + g * NSA_DK:GD + (g + 1) * NSA_DK, :].astype(BF16),
                                _unit_block((LANE - NSA_DK, nk), 0)], axis=0)
        for hl in range(NSA_HPG):
            h = g * NSA_HPG + hl
            q_h = q_ref[0, :, h * NSA_DK:(h + 1) * NSA_DK]
            s = jnp.dot(q_h, kt_g, preferred_element_type=F32) + (bias_ref[h] + hidden)
            m = jnp.maximum(jnp.full((tq, LANE), NEG_INF, F32), jnp.max(s, axis=1)[:, None])
            p = jnp.exp(s - _lane_tiles(m, nk // LANE))
            o = _dot_t(p.astype(BF16), vt_g)
            outs.append(o[:, 0:NSA_DK] / jnp.maximum(o[:, NSA_DK:NSA_DK + 1], TINY) * _gate_col(g3, h, 2))
    o_ref[0] = jnp.concatenate(outs, axis=1)


def _win_attn(q, g3, bias, pieces, *, t0, tq):
    nb, tt, _ = q.shape
    tile = lambda width: pl.BlockSpec((1, tq, width), lambda b, i: (b, i, 0))
    in_specs = [tile(D_MODEL), tile(LANE),
                pl.BlockSpec(bias.shape, lambda b, i: (0, 0, 0), pipeline_mode=pl.Buffered(1))]
    for arr, npos, idx in pieces:
        in_specs.append(pl.BlockSpec((1, 2 * GD, npos), functools.partial(lambda idx, b, i: (b, 0, idx(i)), idx)))
    return pl.pallas_call(
        functools.partial(_win_kernel, t0, tq, len(pieces)),
        grid=(nb, tt // tq),
        in_specs=in_specs,
        out_specs=tile(D_MODEL),
        out_shape=jax.ShapeDtypeStruct((nb, tt, D_MODEL), F32),
        compiler_params=_cparams(("parallel", "arbitrary")),
        name="win_attn",
    )(q, g3, bias, *[a for a, _, _ in pieces])


def _window_bias(tab_nsa_t, tq, nk):
    r = np.arange(tq)[:, None]
    c = np.arange(nk)[None, :]
    return _bias_lookup(tab_nsa_t, _t5_bucket_np(r - c + WINDOW)[None]).reshape(NSA_HEADS, tq, nk)


def _diff_kernel(t0, tq, ppt, has_past, lam_init, cases, *refs):
    tab_ref = refs[0]
    refs = refs[2 if has_past else 1:]
    q_ref, lam_ref, sub_ref, band_ref, kvn_ref = refs[:5]
    refs = refs[5:]
    if has_past:
        kvp_refs = refs[0:ppt]
        refs = refs[ppt:]
    o_ref, m_ref, acc_ref, qq_ref = refs
    p = pl.program_id(1)
    _flash_prologue(tab_ref, p, m_ref, acc_ref)

    @pl.when(tab_ref[T_FIRST, p] == 1)
    def _():
        first_map = lax.broadcasted_iota(jnp.int32, (tq, DIFF_DV), 1) < DIFF_DD
        for h in range(DIFF_HEADS):
            q_h = q_ref[0, :, h * DIFF_DV:(h + 1) * DIFF_DV].astype(F32)
            qq_ref[h] = jnp.concatenate([jnp.where(first_map, q_h, 0.0), jnp.where(first_map, 0.0, q_h)],
                                        axis=0).astype(qq_ref.dtype)

    fuse = DIFF_HEADS * 2 * tq <= FUSE_ROWS

    def step(is_new, bsel):
        tk = kvn_ref.shape[1] if is_new else ppt * PAGE
        s_list, pv_list = [], []
        if not is_new:
            by_head = [jnp.swapaxes(r[0].reshape(PAGE, 2 * DIFF_HEADS, DIFF_DV), 0, 1).astype(BF16)
                       for r in kvp_refs]
            page_rows = lambda row: jnp.concatenate([pg[row] for pg in by_head], axis=0)
        for h in range(DIFF_HEADS):
            if is_new:
                k_h = kvn_ref[0, :, h * DIFF_DV:(h + 1) * DIFF_DV]
                v_h = kvn_ref[0, :, D_MODEL + h * DIFF_DV:D_MODEL + (h + 1) * DIFF_DV]
            else:
                k_h, v_h = page_rows(h), page_rows(DIFF_HEADS + h)
            v_h = jnp.concatenate([v_h, _unit_block((tk, LANE), 1)], axis=1)
            s = _dot_t(qq_ref[h].astype(BF16), k_h)
            if bsel != BIAS_FAR:
                extra = _band_correction(band_ref, h, tq, tk, 0 if bsel == BIAS_DIAG else tk)
                if bsel == BIAS_DIAG:
                    extra = extra + _causal_neg(tq, tk)
                s = s + jnp.concatenate([extra, extra], axis=0)
            pv = functools.partial(lambda v, pb: jnp.dot(pb, v, preferred_element_type=F32), v_h)
            if fuse:
                s_list.append(s)
                pv_list.append(pv)
            else:
                _online_softmax_step(s, pv, m_ref.at[h], acc_ref.at[h])
        if fuse:
            _online_softmax_fused(s_list, pv_list, m_ref, acc_ref)

    _when_cases(tab_ref, p, cases, step)

    @pl.when(tab_ref[T_LAST, p] == 1)
    def _():
        lam = lam_ref[0:1, 0:1]
        for h in range(DIFF_HEADS):
            o = acc_ref[h, :, 0:DIFF_DV] / jnp.maximum(acc_ref[h, :, DIFF_DV:DIFF_DV + 1], TINY)
            o = o[0:tq] - lam * o[tq:2 * tq]
            y = o * lax.rsqrt(jnp.mean(o * o, axis=-1, keepdims=True) + EPS) * sub_ref[...]
            o_ref[0, :, h * DIFF_DV:(h + 1) * DIFF_DV] = y * (1.0 - lam_init)


def _diff_attn(q, lam, subln, band, new_rows, *, t0, tq, tkn, lam_init, past=None):
    nb, tt, _ = q.shape
    tkp = past[2] if past else tkn
    ppt = tkp // PAGE
    tab, cases = _pair_tables(tt // tq, tq, t0, tkp, tkn)
    qtile, const = _flash_specs(tq)
    new_spec = pl.BlockSpec((1, tkn, 2 * D_MODEL), lambda b, p, tab, *_: (b, tab[T_J, p] * tab[T_NEW, p], 0))
    in_specs = [qtile(D_MODEL), const(lam.shape), const(subln.shape), const(band.shape), new_spec]
    args = [q, lam, subln, band, new_rows]
    prefetch = [tab]
    if past:
        page_spec = lambda k: pl.BlockSpec(
            (1, PAGE * 2 * DIFF_HEADS, DIFF_DV), lambda b, p, tab, pt: (_past_page(tab, pt, b, p, ppt, k), 0, 0))
        in_specs += [page_spec(k) for k in range(ppt)]
        args += [past[1]] * ppt
        prefetch.append(past[0])
    return pl.pallas_call(
        functools.partial(_diff_kernel, t0, tq, ppt, past is not None, lam_init, cases),
        grid_spec=pltpu.PrefetchScalarGridSpec(
            num_scalar_prefetch=len(prefetch),
            grid=(nb, tab.shape[1]),
            in_specs=in_specs,
            out_specs=qtile(D_MODEL),
            scratch_shapes=[pltpu.VMEM((DIFF_HEADS, 2 * tq, LANE), F32),
                            pltpu.VMEM((DIFF_HEADS, 2 * tq, DIFF_DV + LANE), F32),
                            pltpu.VMEM((DIFF_HEADS, 2 * tq, DIFF_DV), BF16)]),
        out_shape=jax.ShapeDtypeStruct((nb, tt, D_MODEL), F32),
        compiler_params=_cparams(("parallel", "arbitrary")),
        name="diff_attn",
    )(*prefetch, *args)


def _out_mlp_kernel(x_ref, oc_ref, os_ref, ow_ref, od_ref, ga_ref, gb_ref, wo_ref, g_ref, wu_ref, wd_ref, y_ref):
    o_a = oc_ref[...] + os_ref[...] + ow_ref[...]
    o = ga_ref[...] * o_a + gb_ref[...] * od_ref[...]
    h = x_ref[...] + jnp.dot(o.astype(BF16), wo_ref[...], preferred_element_type=F32)
    hn = (h * lax.rsqrt(jnp.mean(h * h, axis=-1, keepdims=True) + EPS) * g_ref[...]).astype(BF16)
    y = h
    for c in range(D_FF // D_MODEL):
        cols = slice(c * D_MODEL, (c + 1) * D_MODEL)
        u = jnp.maximum(jnp.dot(hn, wu_ref[:, cols], preferred_element_type=F32), 0.0)
        y = y + jnp.dot((u * u).astype(BF16), wd_ref[cols, :], preferred_element_type=F32)
    y_ref[...] = y


def _out_mlp(x2d, oc, os_, ow, od, ga, gb, w_o, mlp_g, w_up, w_down, tm):
    m = x2d.shape[0]
    row = pl.BlockSpec((tm, D_MODEL), lambda i: (i, 0))
    const = lambda shape: pl.BlockSpec(shape, lambda i: (0, 0), pipeline_mode=pl.Buffered(1))
    return pl.pallas_call(
        _out_mlp_kernel,
        grid=(m // tm,),
        in_specs=[row] * 7 + [const(w_o.shape), const(mlp_g.shape), const(w_up.shape), const(w_down.shape)],
        out_specs=row,
        out_shape=jax.ShapeDtypeStruct((m, D_MODEL), F32),
        compiler_params=_cparams(("parallel",)),
        name="out_mlp",
    )(x2d, oc, os_, ow, od, ga, gb, w_o, mlp_g, w_up, w_down)


def _win_shift_kernel(n_new, past_ref, new_ref, o_ref):
    keep = o_ref.shape[2]
    o_ref[0] = jnp.concatenate([past_ref[0, :, n_new:keep], new_ref[0, :, 0:n_new]], axis=1)


def _win_shift(past_t, new_t, n_new):
    nb, feats, keep = past_t.shape
    return pl.pallas_call(
        functools.partial(_win_shift_kernel, n_new),
        grid=(nb,),
        in_specs=[pl.BlockSpec((1, feats, keep), lambda b: (b, 0, 0)),
                  pl.BlockSpec((1, feats, new_t.shape[2]), lambda b: (b, 0, 0))],
        out_specs=pl.BlockSpec((1, feats, keep), lambda b: (b, 0, 0)),
        out_shape=jax.ShapeDtypeStruct(past_t.shape, past_t.dtype),
        compiler_params=_cparams(("parallel",)),
        name="win_shift",
    )(past_t, new_t)


TQ_PROMPT = 256
TQ_FLASH = 512
TM_PROMPT = 256
TQ_SAMPLE = 8
TK_PAST = 2048


def _round_up(x, m):
    return -(-x // m) * m


def _feature_major(cache):
    lead = cache.shape[:-4]
    p, s, g, dk = cache.shape[-4:]
    n = len(lead)
    return jnp.transpose(cache, tuple(range(n)) + (n + 1, n + 2, n + 3, n)).reshape(lead + (s * g * dk, p))


def _position_major(feat, slots):
    lead = feat.shape[:-2]
    p = feat.shape[-1]
    n = len(lead)
    x = feat.reshape(lead + (slots, NSA_GROUPS, NSA_DK, p))
    return jnp.transpose(x, tuple(range(n)) + (n + 3, n, n + 1, n + 2))


def _layer_params(l, lam_init, rel_bias_table, attn_norm, w_in, nsa_q_gain, nsa_k_gain,
                  cmp_pe_k, cmp_w1_k, cmp_w2_k, cmp_pe_v, cmp_w1_v, cmp_w2_v,
                  diff_q_gain, diff_k_gain, lambda_q1, lambda_k1, lambda_q2, lambda_k2, diff_subln,
                  w_o, mlp_norm, w_up, w_down):
    tile4 = lambda g: jnp.tile(g.astype(F32), MXU_DIM // NSA_DK)
    gains = jnp.zeros((8, MXU_DIM), F32)
    for r, g in enumerate((nsa_q_gain[l], diff_q_gain[l], diff_k_gain[l])):
        gains = gains.at[r].set(tile4(g))
    flat_pe = lambda pe: jnp.broadcast_to(pe.reshape(1, CMP_BLOCK * NSA_DK), (8, CMP_BLOCK * NSA_DK))
    w1 = jnp.stack([cmp_w1_k[l], cmp_w1_v[l]])
    w2 = jnp.stack([cmp_w2_k[l], cmp_w2_v[l]])
    w2z = jnp.zeros((2, NSA_GROUPS, CMP_HID, GD), F32)
    for g in range(NSA_GROUPS):
        w2z = w2z.at[:, g, :, g * NSA_DK:(g + 1) * NSA_DK].set(w2)
    lam = (jnp.exp(jnp.sum(lambda_q1[l].astype(F32) * lambda_k1[l].astype(F32)))
           - jnp.exp(jnp.sum(lambda_q2[l].astype(F32) * lambda_k2[l].astype(F32))) + lam_init)
    w_row, w_kvt = _split_w_in(w_in[l])
    k_gain = tile4(nsa_k_gain[l])
    return dict(
        attn_g=attn_norm[l].reshape(1, D_MODEL), w_row=w_row, w_kvt=w_kvt, bd=_segment_mean_matrix(),
        gains=gains, k_gain=k_gain.reshape(1, GD), k_gain_col=k_gain.reshape(GD, 1),
        pe2=jnp.stack([flat_pe(cmp_pe_k[l]), flat_pe(cmp_pe_v[l])]), w1=w1, w1_bf=w1.astype(BF16),
        w2z=w2z.astype(BF16),
        tab_nsa_t=rel_bias_table[:, :NSA_HEADS].astype(F32).T, tab_diff_t=rel_bias_table[:, NSA_HEADS:].astype(F32).T,
        lam=jnp.broadcast_to(lam.reshape(1, 1), (8, LANE)), subln=diff_subln[l].reshape(1, DIFF_DV),
        w_o=w_o[l].astype(BF16), mlp_g=mlp_norm[l].reshape(1, D_MODEL),
        w_up=w_up[l].astype(BF16), w_down=w_down[l].astype(BF16))


def _run_proj(P, x, tm):
    kgain_t = jnp.broadcast_to(P['k_gain_col'], (GD, tm))
    return _proj(x, P['attn_g'], P['w_row'], P['w_kvt'], P['bd'], P['gains'], kgain_t, tm)


def _mixers(P, lam_init, q, g3, qd, sel_t, win_pieces, kvd, cmp_pages, tail, *,
            t0, tq, tqf, tkn, seq_len, nsa_past, diff_past):
    nb = q.shape[0]
    total = t0 + seq_len
    h0 = _pe_terms(P['pe2'], P['w1'])
    kc, vc = _compress(cmp_pages[0], cmp_pages[1], nb, cmp_pages[2], tail, P['w1_bf'], h0, P['w2z'], P['bd'],
                       P['k_gain'], page_table=cmp_pages[3])
    ncp = vc.shape[1]
    nc = -(-total // CMP_STRIDE) - CMP_BLOCK // CMP_STRIDE + 1
    n_slc = -(-total // SEL_BLOCK)
    nsp = _round_up(n_slc, LANE)
    bias_c, selwt = _cmp_tables(P['tab_nsa_t'], tqf, ncp, n_slc, nsp)
    o_cmp, msk = _cmp_topk(q, kc, vc, g3, bias_c, selwt, t0=t0, tq=tqf, nc=nc, n_slc=n_slc, n_top=min(N_SEL, n_slc))
    sel_past = dpast = None
    if nsa_past is not None:
        sel_past = (nsa_past[0], nsa_past[1], TK_PAST)
        dpast = (diff_past[0], diff_past[1], TK_PAST)
    o_sel = _sel_attn(q, msk, g3, _band_bias(P['tab_nsa_t']), sel_t, t0=t0, tq=tqf, tkn=tkn, past=sel_past)
    nk = sum(n for _, n, _ in win_pieces)
    o_win = _win_attn(q, g3, _window_bias(P['tab_nsa_t'], tq, nk), win_pieces, t0=t0, tq=tq)
    tkd = kvd.shape[1] if diff_past is not None else tkn
    o_diff = _diff_attn(qd, P['lam'], P['subln'], _band_bias(P['tab_diff_t']), kvd,
                        t0=t0, tq=tqf, tkn=tkd, lam_init=lam_init, past=dpast)
    return o_cmp, o_sel, o_win, o_diff


def _prompt_layer(P, lam_init, x):
    b, t, _ = x.shape
    tq = TQ_PROMPT
    q, nsa_t, win_t, g3, qd, diff5, ga, gb, kvd, sel_t = _run_proj(P, x, TM_PROMPT)
    back = WINDOW // tq
    pieces = [(win_t, tq, functools.partial(lambda k, i: jnp.maximum(i - k, 0), k)) for k in range(back, -1, -1)]
    tail = jnp.zeros((b, 2 * GD, PAGE), F32)
    cmp_pages = (nsa_t, lambda bb, page, pt: (bb, 0, page), t // PAGE, None)
    o_cmp, o_sel, o_win, o_diff = _mixers(
        P, lam_init, q, g3, qd, sel_t, pieces, kvd, cmp_pages, tail,
        t0=0, tq=tq, tqf=min(TQ_FLASH, t), tkn=min(TQ_FLASH, t), seq_len=t, nsa_past=None, diff_past=None)
    f2 = lambda a: a.reshape(b * t, D_MODEL)
    y = _out_mlp(f2(x), f2(o_cmp), f2(o_sel), f2(o_win), f2(o_diff), f2(ga), f2(gb),
                 P['w_o'], P['mlp_g'], P['w_up'], P['w_down'], TM_PROMPT)
    keep = min(WINDOW, t)
    return y.reshape(b, t, D_MODEL), _position_major(nsa_t, 4), diff5, _position_major(win_t[:, :, t - keep:], 2)


def _sample_layer(P, lam_init, x, cache_nsa, cache_diff, cache_win, page_table):
    nb, t, _ = x.shape
    tq = TQ_SAMPLE
    assert t <= tq and cache_win.shape[1] == WINDOW
    past_len = page_table.shape[1] * PAGE
    rows = nb * t
    q, nsa_t, win_t, g3, qd, diff5, ga, gb, kvd, sel_t = _run_proj(P, x.reshape(1, rows, D_MODEL), rows)
    per_seq = lambda a: jnp.pad(a.reshape(nb, t, a.shape[-1]), ((0, 0), (0, tq - t), (0, 0)))
    per_seq_t = lambda a: jnp.pad(jnp.transpose(a.reshape(a.shape[1], nb, t), (1, 0, 2)),
                                  ((0, 0), (0, 0), (0, PAGE - t)))
    nsa_new_t, win_new_t = per_seq_t(nsa_t), per_seq_t(win_t)
    pages_t = _feature_major(cache_nsa)
    past_win_t = _feature_major(cache_win)
    diff_pages = cache_diff.reshape(cache_diff.shape[0], PAGE * 2 * DIFF_HEADS, DIFF_DV)
    pieces = [(past_win_t, WINDOW, lambda i: 0), (win_new_t, PAGE, lambda i: 0)]
    cmp_pages = (pages_t, lambda bb, page, pt: (pt[bb, page], 0, 0), page_table.shape[1], page_table)
    o_cmp, o_sel, o_win, o_diff = _mixers(
        P, lam_init, per_seq(q[0]), per_seq(g3[0]), per_seq(qd[0]), per_seq_t(sel_t), pieces,
        jnp.pad(kvd.reshape(nb, t, 2 * D_MODEL), ((0, 0), (0, PAGE - t), (0, 0))), cmp_pages, nsa_new_t,
        t0=past_len, tq=tq, tqf=tq, tkn=PAGE, seq_len=t,
        nsa_past=(page_table, pages_t), diff_past=(page_table, diff_pages))
    f2 = lambda a: a[:, :t].reshape(rows, D_MODEL)
    y = _out_mlp(x.reshape(rows, D_MODEL), f2(o_cmp), f2(o_sel), f2(o_win), f2(o_diff), ga[0], gb[0],
                 P['w_o'], P['mlp_g'], P['w_up'], P['w_down'], rows)
    win_out_t = _win_shift(past_win_t, win_new_t, t)
    nsa_rows = _position_major(nsa_new_t[:, :, :t], 4)
    return (y.reshape(nb, t, D_MODEL), nsa_rows, diff5.reshape(nb, t, 2, DIFF_HEADS, DIFF_DV),
            _position_major(win_out_t, 2))


def kernel(x_prompt, x_sample, cache_nsa_kv, cache_diff_kv, cache_nsa_win, page_table, rel_bias_table, attn_norm, w_in, nsa_q_gain, nsa_k_gain, cmp_pe_k, cmp_w1_k, cmp_w2_k, cmp_pe_v, cmp_w1_v, cmp_w2_v, diff_q_gain, diff_k_gain, lambda_q1, lambda_k1, lambda_q2, lambda_k2, diff_subln, w_o, mlp_norm, w_up, w_down):
    depth = w_in.shape[0]
    yp, ys = x_prompt, x_sample
    outs = [[] for _ in range(6)]
    for l in range(depth):
        lam_init = 0.8 - 0.6 * math.exp(-0.3 * l)
        P = _layer_params(l, lam_init, rel_bias_table, attn_norm, w_in, nsa_q_gain, nsa_k_gain,
                          cmp_pe_k, cmp_w1_k, cmp_w2_k, cmp_pe_v, cmp_w1_v, cmp_w2_v,
                          diff_q_gain, diff_k_gain, lambda_q1, lambda_k1, lambda_q2, lambda_k2, diff_subln,
                          w_o, mlp_norm, w_up, w_down)
        yp, nsa_p, diff_p, win_p = _prompt_layer(P, lam_init, yp)
        ys, nsa_s, diff_s, win_s = _sample_layer(P, lam_init, ys, cache_nsa_kv[l], cache_diff_kv[l],
                                                 cache_nsa_win[l], page_table)
        for dst, v in zip(outs, (nsa_p, diff_p, win_p, nsa_s, diff_s, win_s)):
            dst.append(v)
    return (yp, ys) + tuple(jnp.stack(o, 0) for o in outs)
```
